```python
import math, functools
import jax, jax.numpy as jnp
from jax import lax
import numpy as np

D_MODEL = 1024
BATCH = 32
SEQ = 2048
DEPTH = 2
DEC_BATCH = 128
DEC_SEQ = 4
PAST_LEN = 16384
PAGE_SIZE = 128

N_EVEN = (DEPTH + 1) // 2
N_ODD = DEPTH // 2
MLA_HEADS = 8
MLA_NOPE = 64
MLA_ROPE = 32
MLA_V = 64
Q_LORA = 256
KV_LORA = 128
DIFF_HEADS = 4
DIFF_KV_HEADS = 2
DIFF_HD = 64
FOX_HEADS = 16
FOX_KV_HEADS = 4
FOX_HD = 64
FORGET_BIAS = 2.0
D_FF = 4 * D_MODEL

ROPE_THETA = 10000.0
Q_BLOCK = 128
ALPHA = (2 * DEPTH) ** 0.25
BETA = (8 * DEPTH) ** -0.25
RMS_EPS = 1e-6
LN_EPS = 1e-5
MLA_SCALE = (MLA_NOPE + MLA_ROPE) ** -0.5
DIFF_SCALE = DIFF_HD ** -0.5
FOX_SCALE = FOX_HD ** -0.5

EVEN_SPLITS = [Q_LORA, KV_LORA, MLA_ROPE, DIFF_HEADS * 2 * DIFF_HD,
               DIFF_KV_HEADS * 2 * DIFF_HD, DIFF_KV_HEADS * 2 * DIFF_HD]
EVEN_IN = sum(EVEN_SPLITS)
EVEN_MIX = MLA_HEADS * MLA_V + DIFF_HEADS * 2 * DIFF_HD
ODD_SPLITS = [FOX_HEADS * FOX_HD, FOX_KV_HEADS * FOX_HD, FOX_KV_HEADS * FOX_HD, FOX_HEADS]
ODD_IN = sum(ODD_SPLITS)
ODD_MIX = FOX_HEADS * FOX_HD

kernel_name = 'hybrid_mla_diff_fox_decoder_step'


def split_cols(y, widths):
    return jnp.split(y, np.cumsum(widths)[:-1].tolist(), axis=-1)


def rms_norm(x, g):
    xf = x.astype(jnp.float32)
    y = xf * lax.rsqrt(jnp.mean(xf * xf, -1, keepdims=True) + RMS_EPS)
    return (y * g).astype(x.dtype)


def layer_norm(x, g, b):
    xf = x.astype(jnp.float32)
    mu = jnp.mean(xf, -1, keepdims=True)
    var = jnp.mean(jnp.square(xf - mu), -1, keepdims=True)
    return ((xf - mu) * lax.rsqrt(var + LN_EPS) * g + b).astype(x.dtype)


def rope(x, pos):
    half = x.shape[-1] // 2
    inv = ROPE_THETA ** (-jnp.arange(half, dtype=jnp.float32) / half)
    ang = pos.astype(jnp.float32)[:, None] * inv[None, :]
    cos = jnp.cos(ang)[:, None, :]
    sin = jnp.sin(ang)[:, None, :]
    x1, x2 = x[..., :half], x[..., half:]
    return jnp.concatenate([x1 * cos - x2 * sin, x2 * cos + x1 * sin], -1).astype(x.dtype)


def attend(q, k, v, q_pos, k_pos, scale, fq=None, fk=None):
    B, Tq, H, dk = q.shape
    G = k.shape[2]
    R = H // G
    s = jnp.einsum('bqgrd,bkgd->bgrqk', q.reshape(B, Tq, G, R, dk), k).astype(jnp.float32) * scale
    if fq is not None:
        bq = fq.astype(jnp.float32).reshape(B, Tq, G, R).transpose(0, 2, 3, 1)[..., None]
        bk = fk.astype(jnp.float32).reshape(B, -1, G, R).transpose(0, 2, 3, 1)[..., None, :]
        s = s + (bq - bk)
    s = jnp.where(k_pos[None, :] <= q_pos[:, None], s, -jnp.inf)
    p = jax.nn.softmax(s, axis=-1).astype(v.dtype)
    o = jnp.einsum('bgrqk,bkgd->bqgrd', p, v)
    return o.reshape(B, Tq, H, v.shape[-1])


def mix(core, q_args, rows, pools, layer, page_table, key_prep=None):
    prep = key_prep if key_prep is not None else (lambda r: r)
    if pools is None:
        S = q_args[0].shape[1]
        nb = S // Q_BLOCK
        rows_p = prep(rows)
        k_pos = jnp.arange(S)
        blocks = tuple(a.reshape((a.shape[0], nb, Q_BLOCK) + a.shape[2:]).swapaxes(0, 1) for a in q_args)

        def body(xs):
            i, qb = xs
            return core(qb, rows_p, i * Q_BLOCK + jnp.arange(Q_BLOCK), k_pos)
        out = lax.map(body, (jnp.arange(nb), blocks))
        return out.swapaxes(0, 1).reshape((out.shape[1], S) + out.shape[3:])
    T = q_args[0].shape[1]
    past_len = page_table.shape[1] * PAGE_SIZE
    q_pos = past_len + jnp.arange(T)
    k_pos = jnp.arange(past_len + T)

    def body(xs):
        qs, news, pages = xs
        full = tuple(jnp.concatenate([p[layer, pages].reshape((past_len,) + p.shape[3:]), n], 0)[None]
                     for p, n in zip(pools, news))
        return core(tuple(a[None] for a in qs), prep(full), q_pos, k_pos)[0]
    return lax.map(body, (q_args, rows, page_table))


def mla_core(qs, rows, q_pos, k_pos):
    (q,) = qs
    (kv,) = rows
    return attend(q, kv[:, :, None, :], kv[:, :, None, :KV_LORA], q_pos, k_pos, MLA_SCALE)


def diff_core(lam, qs, rows, q_pos, k_pos):
    (q,) = qs
    (kv,) = rows
    k1, k2, v = kv[..., :DIFF_HD], kv[..., DIFF_HD:2 * DIFF_HD], kv[..., 2 * DIFF_HD:]
    a1 = attend(q[:, :, :, 0], k1, v, q_pos, k_pos, DIFF_SCALE)
    a2 = attend(q[:, :, :, 1], k2, v, q_pos, k_pos, DIFF_SCALE)
    return a1 - lam.astype(a1.dtype) * a2


def fox_key_prep(rows):
    kv, logf = rows
    logf = logf.astype(jnp.float32)
    return (kv, logf - lax.cumsum(logf, axis=1, reverse=True))


def fox_core(qs, rows, q_pos, k_pos):
    (q,) = qs
    kv, fk = rows
    fq = jnp.take(fk, q_pos, axis=1)
    return attend(q, kv[..., :FOX_HD], kv[..., FOX_HD:], q_pos, k_pos, FOX_SCALE, fq, fk)


def even_mixer(h, pos, w, i, l, pools, page_table):
    B, T, _ = h.shape
    c_q, c_kv, k_pe, dq, dk, dv = split_cols(h @ w['even_w_in'][i], EVEN_SPLITS)
    q = jnp.einsum('btc,che->bthe', rms_norm(c_q, w['mla_g_q'][i]), w['mla_w_uq'][i])
    q_pe = rope(q[..., MLA_NOPE:], pos)
    q_lat = jnp.einsum('bthd,chd->bthc', q[..., :MLA_NOPE], w['mla_w_uk'][i])
    q_mla = jnp.concatenate([q_lat, q_pe], -1)
    mla_rows = jnp.concatenate([rms_norm(c_kv, w['mla_g_kv'][i]),
                                rope(k_pe[:, :, None, :], pos)[:, :, 0]], -1)
    dq = rope(dq.reshape(B, T, 2 * DIFF_HEADS, DIFF_HD), pos).reshape(B, T, DIFF_HEADS, 2, DIFF_HD)
    dk = rope(dk.reshape(B, T, 2 * DIFF_KV_HEADS, DIFF_HD), pos).reshape(B, T, DIFF_KV_HEADS, 2 * DIFF_HD)
    diff_rows = jnp.concatenate([dk, dv.reshape(B, T, DIFF_KV_HEADS, 2 * DIFF_HD)], -1)
    lam_init = 0.8 - 0.6 * math.exp(-0.3 * l)
    lam = (jnp.exp(jnp.sum((w['diff_lam_q1'][i] * w['diff_lam_k1'][i]).astype(jnp.float32)))
           - jnp.exp(jnp.sum((w['diff_lam_q2'][i] * w['diff_lam_k2'][i]).astype(jnp.float32))) + lam_init)
    p_mla = None if pools is None else (pools[0],)
    p_diff = None if pools is None else (pools[1],)
    lat = mix(mla_core, (q_mla,), (mla_rows,), p_mla, i, page_table)
    dif = mix(functools.partial(diff_core, lam), (dq,), (diff_rows,), p_diff, i, page_table)
    o_mla = jnp.einsum('bthc,chd->bthd', lat, w['mla_w_uv'][i]).reshape(B, T, MLA_HEADS * MLA_V)
    o_dif = (rms_norm(dif, w['diff_g_sub'][i]) * (1.0 - lam_init)).reshape(B, T, DIFF_HEADS * 2 * DIFF_HD)
    y = jnp.concatenate([o_mla, o_dif], -1) @ w['even_w_out'][i]
    return y, mla_rows, diff_rows


def odd_mixer(h, w, i, pools, page_table):
    B, T, _ = h.shape
    q, k, v, f = split_cols(h @ w['fox_w_in'][i], ODD_SPLITS)
    q = q.reshape(B, T, FOX_HEADS, FOX_HD)
    kv_rows = jnp.concatenate([k.reshape(B, T, FOX_KV_HEADS, FOX_HD),
                               v.reshape(B, T, FOX_KV_HEADS, FOX_HD)], -1)
    logf = jax.nn.log_sigmoid((f + w['fox_b_f'][i]).astype(jnp.float32))
    o = mix(fox_core, (q,), (kv_rows, logf), pools, i, page_table, fox_key_prep)
    y = o.reshape(B, T, ODD_MIX) @ w['fox_w_out'][i]
    return y, kv_rows, logf


def trunk(x, c, pos, pools_even, pools_odd, page_table, w):
    r_mla, r_diff, r_kv, r_lf = [], [], [], []
    for l in range(DEPTH):
        mod = jnp.einsum('bd,de->be', jax.nn.silu(c), w['ada_w'][l]) + w['ada_b'][l]
        sh1, sc1, g1, sh2, sc2, g2 = jnp.split(mod[:, None, :], 6, axis=-1)
        h = x * (1.0 + sc1) + sh1
        if l % 2 == 0:
            y, rm, rd = even_mixer(h, pos, w, l // 2, l, pools_even, page_table)
            r_mla.append(rm)
            r_diff.append(rd)
        else:
            y, rk, rf = odd_mixer(h, w, l // 2, pools_odd, page_table)
            r_kv.append(rk)
            r_lf.append(rf)
        x = layer_norm(ALPHA * x + (1.0 + g1) * y, w['ln_g'][l, 0], w['ln_b'][l, 0])
        h = x * (1.0 + sc2) + sh2
        y = jnp.square(jax.nn.relu(h @ w['mlp_w_up'][l])) @ w['mlp_w_down'][l]
        x = layer_norm(ALPHA * x + (1.0 + g2) * y, w['ln_g'][l, 1], w['ln_b'][l, 1])
    return x, jnp.stack(r_mla), jnp.stack(r_diff), jnp.stack(r_kv), jnp.stack(r_lf)


def setup_inputs(seed: int = 0) -> dict:
    key = jax.random.key(seed)
    ks = iter(jax.random.split(key, 40))
    n_pages = PAST_LEN // PAGE_SIZE
    n_pool = (DEC_BATCH * n_pages * 5) // 4

    def normal(shape, scale):
        return jax.random.normal(next(ks), shape, jnp.float32) * scale

    def gain(shape):
        return 1.0 + normal(shape, 0.1)

    x_prompt = normal((BATCH, SEQ, D_MODEL), 1.0)
    x_sample = normal((DEC_BATCH, DEC_SEQ, D_MODEL), 1.0)
    cache_mla = normal((N_EVEN, n_pool, PAGE_SIZE, KV_LORA + MLA_ROPE), 1.0)
    cache_diff = normal((N_EVEN, n_pool, PAGE_SIZE, DIFF_KV_HEADS, 4 * DIFF_HD), 1.0)
    cache_fox_kv = normal((N_ODD, n_pool, PAGE_SIZE, FOX_KV_HEADS, 2 * FOX_HD), 1.0)
    cache_fox_logf = jax.nn.log_sigmoid(FORGET_BIAS + normal((N_ODD, n_pool, PAGE_SIZE, FOX_HEADS), 1.0))
    page_table = jax.random.permutation(next(ks), n_pool)[:DEC_BATCH * n_pages].reshape(
        DEC_BATCH, n_pages).astype(jnp.int32)
    c_prompt = normal((BATCH, D_MODEL), 1.0)
    c_sample = normal((DEC_BATCH, D_MODEL), 1.0)
    return {
        'x_prompt': x_prompt, 'x_sample': x_sample,
        'cache_mla': cache_mla, 'cache_diff': cache_diff,
        'cache_fox_kv': cache_fox_kv, 'cache_fox_logf': cache_fox_logf,
        'page_table': page_table, 'c_prompt': c_prompt, 'c_sample': c_sample,
        'even_w_in': normal((N_EVEN, D_MODEL, EVEN_IN), D_MODEL ** -0.5),
        'mla_g_q': gain((N_EVEN, Q_LORA)),
        'mla_w_uq': normal((N_EVEN, Q_LORA, MLA_HEADS, MLA_NOPE + MLA_ROPE), Q_LORA ** -0.5),
        'mla_g_kv': gain((N_EVEN, KV_LORA)),
        'mla_w_uk': normal((N_EVEN, KV_LORA, MLA_HEADS, MLA_NOPE), KV_LORA ** -0.5),
        'mla_w_uv': normal((N_EVEN, KV_LORA, MLA_HEADS, MLA_V), KV_LORA ** -0.5),
        'diff_lam_q1': normal((N_EVEN, DIFF_HD), 0.1),
        'diff_lam_k1': normal((N_EVEN, DIFF_HD), 0.1),
        'diff_lam_q2': normal((N_EVEN, DIFF_HD), 0.1),
        'diff_lam_k2': normal((N_EVEN, DIFF_HD), 0.1),
        'diff_g_sub': gain((N_EVEN, 2 * DIFF_HD)),
        'even_w_out': normal((N_EVEN, EVEN_MIX, D_MODEL), BETA * EVEN_MIX ** -0.5),
        'fox_w_in': normal((N_ODD, D_MODEL, ODD_IN), D_MODEL ** -0.5),
        'fox_b_f': FORGET_BIAS + normal((N_ODD, FOX_HEADS), 0.5),
        'fox_w_out': normal((N_ODD, ODD_MIX, D_MODEL), BETA * ODD_MIX ** -0.5),
        'ada_w': normal((DEPTH, D_MODEL, 6 * D_MODEL), 0.1 * D_MODEL ** -0.5),
        'ada_b': normal((DEPTH, 6 * D_MODEL), 0.02),
        'ln_g': gain((DEPTH, 2, D_MODEL)),
        'ln_b': normal((DEPTH, 2, D_MODEL), 0.02),
        'mlp_w_up': normal((DEPTH, D_MODEL, D_FF), D_MODEL ** -0.5),
        'mlp_w_down': normal((DEPTH, D_FF, D_MODEL), BETA * D_FF ** -0.5),
    }


def reference(x_prompt, x_sample, cache_mla, cache_diff, cache_fox_kv, cache_fox_logf, page_table,
              c_prompt, c_sample, even_w_in, mla_g_q, mla_w_uq, mla_g_kv, mla_w_uk, mla_w_uv,
              diff_lam_q1, diff_lam_k1, diff_lam_q2, diff_lam_k2, diff_g_sub, even_w_out,
              fox_w_in, fox_b_f, fox_w_out, ada_w, ada_b, ln_g, ln_b, mlp_w_up, mlp_w_down):
    w = {
        'even_w_in': even_w_in, 'mla_g_q': mla_g_q, 'mla_w_uq': mla_w_uq, 'mla_g_kv': mla_g_kv,
        'mla_w_uk': mla_w_uk, 'mla_w_uv': mla_w_uv, 'diff_lam_q1': diff_lam_q1,
        'diff_lam_k1': diff_lam_k1, 'diff_lam_q2': diff_lam_q2, 'diff_lam_k2': diff_lam_k2,
        'diff_g_sub': diff_g_sub, 'even_w_out': even_w_out, 'fox_w_in': fox_w_in,
        'fox_b_f': fox_b_f, 'fox_w_out': fox_w_out, 'ada_w': ada_w, 'ada_b': ada_b,
        'ln_g': ln_g, 'ln_b': ln_b, 'mlp_w_up': mlp_w_up, 'mlp_w_down': mlp_w_down,
    }
    pos_prompt = jnp.arange(x_prompt.shape[1])
    y_prompt, mla_p, diff_p, fox_kv_p, fox_lf_p = trunk(x_prompt, c_prompt, pos_prompt,
                                                        None, None, None, w)
    past_len = page_table.shape[1] * PAGE_SIZE
    pos_sample = past_len + jnp.arange(x_sample.shape[1])
    y_sample, mla_s, diff_s, fox_kv_s, fox_lf_s = trunk(x_sample, c_sample, pos_sample,
                                                        (cache_mla, cache_diff),
                                                        (cache_fox_kv, cache_fox_logf), page_table, w)
    return (y_prompt, y_sample, mla_p, mla_s, diff_p, diff_s, fox_kv_p, fox_kv_s, fox_lf_p, fox_lf_s)
```

```python
import functools
import math

import numpy as np
import jax
import jax.numpy as jnp
from jax import lax
from jax.experimental import pallas as pl
from jax.experimental.pallas import tpu as pltpu

F32, BF16, I32 = jnp.float32, jnp.bfloat16, jnp.int32
SDS = jax.ShapeDtypeStruct

D_MODEL = 1024
PAGE_SIZE = 128
MLA_HEADS, MLA_NOPE, MLA_ROPE, MLA_V = 8, 64, 32, 64
Q_LORA, KV_LORA = 256, 128
DIFF_HEADS, DIFF_KV_HEADS, DIFF_HD = 4, 2, 64
FOX_HEADS, FOX_KV_HEADS, FOX_HD = 16, 4, 64
D_FF = 4 * D_MODEL
ROPE_THETA = 10000.0
RMS_EPS = 1e-6
LN_EPS = 1e-5
MLA_SCALE = (MLA_NOPE + MLA_ROPE) ** -0.5
DIFF_SCALE = DIFF_HD ** -0.5
FOX_SCALE = FOX_HD ** -0.5
EVEN_SPLITS = (Q_LORA, KV_LORA, MLA_ROPE, DIFF_HEADS * 2 * DIFF_HD, DIFF_KV_HEADS * 2 * DIFF_HD,
               DIFF_KV_HEADS * 2 * DIFF_HD)
ODD_SPLITS = (FOX_HEADS * FOX_HD, FOX_KV_HEADS * FOX_HD, FOX_KV_HEADS * FOX_HD, FOX_HEADS)

LANES = 128
SUBLANES = 8
VMEM_LIMIT_BYTES = 56 * 1024 * 1024

LOG2E = 1.4426950408889634
NEG_BIG = -1e30
MLA_KW = 256
MLA_KUSED = KV_LORA + MLA_ROPE
FOX_HI, FOX_MID, FOX_LO, FOX_ONE = 64, 80, 96, 112
NEW_ROWS_PAD = 8


def _params(sem):
    return pltpu.CompilerParams(dimension_semantics=sem, vmem_limit_bytes=VMEM_LIMIT_BYTES)


def _const_spec(shape):
    nd = len(shape)
    return pl.BlockSpec(shape, lambda *_: (0,) * nd, pipeline_mode=pl.Buffered(1))


def _rms(x, g):
    return x * lax.rsqrt(jnp.mean(x * x, axis=-1, keepdims=True) + RMS_EPS) * g


def _layer_norm(x, g, b):
    mu = jnp.mean(x, axis=-1, keepdims=True)
    xc = x - mu
    var = jnp.mean(xc * xc, axis=-1, keepdims=True)
    return xc * lax.rsqrt(var + LN_EPS) * g + b


def _rope_chunk(x, cos, sin_signed, half):
    fwd = pltpu.roll(x, LANES - half, 1)
    bwd = pltpu.roll(x, half, 1)
    lane = lax.broadcasted_iota(I32, x.shape, 1)
    swapped = jnp.where((lane & half) == 0, fwd, bwd)
    return x * cos + swapped * sin_signed


def _split3(x):
    hi = x.astype(BF16)
    r1 = x - hi.astype(F32)
    mid = r1.astype(BF16)
    lo = (r1 - mid.astype(F32)).astype(BF16)
    return hi, mid, lo


def _dot(a, b):
    return jnp.dot(a, b, preferred_element_type=F32)


def _dot_nt(a, b):
    return lax.dot_general(a, b, (((1,), (1,)), ((), ())), preferred_element_type=F32)


def _softmax_update(s, pv, m_ref, l_ref, acc_ref):
    m_prev = m_ref[...]
    m_new = jnp.maximum(m_prev, jnp.max(s, axis=1, keepdims=True))
    alpha = jnp.exp2(m_prev - m_new)
    p = jnp.exp2(s - m_new)
    l_ref[...] = alpha * l_ref[...] + jnp.sum(p, axis=1, keepdims=True)
    acc_ref[...] = alpha * acc_ref[...] + pv(p.astype(BF16))
    m_ref[...] = m_new


def _init_softmax(m_ref, l_ref, acc_ref):
    m_ref[...] = jnp.full(m_ref.shape, NEG_BIG, F32)
    l_ref[...] = jnp.zeros(l_ref.shape, F32)
    acc_ref[...] = jnp.zeros(acc_ref.shape, F32)


def _adaln_kernel(c_ref, w_ref, b_ref, o_ref):
    c = c_ref[...]
    s = (c * jax.nn.sigmoid(c)).astype(BF16)
    o_ref[...] = _dot(s, w_ref[...].astype(BF16)) + b_ref[...]


def _adaln(c_all, ada_w, ada_b):
    n_layers, d, d6 = ada_w.shape
    bc = c_all.shape[0]
    tn = d6 // 4
    return pl.pallas_call(
        _adaln_kernel,
        out_shape=SDS((n_layers, bc, d6), F32),
        grid=(n_layers, d6 // tn),
        in_specs=[pl.BlockSpec((bc, d), lambda l, j: (0, 0)),
                  pl.BlockSpec((None, d, tn), lambda l, j: (l, 0, j)),
                  pl.BlockSpec((None, 1, tn), lambda l, j: (l, 0, j))],
        out_specs=pl.BlockSpec((None, bc, tn), lambda l, j: (l, 0, j)),
        compiler_params=_params(("arbitrary", "arbitrary")),
        name="adaln",
    )(c_all, ada_w, ada_b.reshape(n_layers, 1, d6))


class _Tokens:
    def __init__(self, n_tok, tile, tiles_per_seq, per_token):
        self.n = n_tok
        self.tm = tile
        self.tps = tiles_per_seq
        self.per_token = per_token
        self.grid = (n_tok // tile,)

    def rows(self, width):
        return pl.BlockSpec((self.tm, width), lambda i: (i, 0))

    def mod(self, chunk):
        if self.per_token:
            return pl.BlockSpec((None, self.tm, D_MODEL), lambda i: (0, i, chunk))
        tps = self.tps
        return pl.BlockSpec((None, 1, D_MODEL), lambda i: (i // tps, 0, chunk))

    def pos(self):
        if self.per_token:
            return pl.BlockSpec((self.tm, LANES), lambda i: (i, 0))
        tps = self.tps
        return pl.BlockSpec((self.tm, LANES), lambda i: (i % tps, 0))


def _rope_tables(pos, half):
    inv = ROPE_THETA ** (-jnp.arange(half, dtype=F32) / half)
    ang = pos.astype(F32)[:, None] * inv[None, :]
    cos, sin = jnp.cos(ang), jnp.sin(ang)
    reps = LANES // (2 * half)
    return (jnp.tile(jnp.concatenate([cos, cos], -1), (1, reps)),
            jnp.tile(jnp.concatenate([-sin, sin], -1), (1, reps)))


_EV_CQ, _EV_CKV, _EV_DQ, _EV_DK, _EV_DV, _EV_KPE, _EV_END = 0, 256, 384, 896, 1152, 1408, 1536
_UQ_ROPE = MLA_HEADS * LANES


def _even_in_kernel(x_ref, sh_ref, sc_ref, c64_ref, s64_ref, c32_ref, s32_ref, win_ref, gq_ref, gkv_ref,
                    wuq_ref, wuk_ref, qmla_ref, kmla_ref, mrows_ref, dq_ref, drows_ref, dkv_ref):
    h = (x_ref[...] * (1.0 + sc_ref[...]) + sh_ref[...]).astype(BF16)
    y = _dot(h, win_ref[...])
    c64, s64, c32, s32 = c64_ref[...], s64_ref[...], c32_ref[...], s32_ref[...]
    lane = lax.broadcasted_iota(I32, c64.shape, 1)
    q_scale = MLA_SCALE * LOG2E

    cq = _rms(y[:, _EV_CQ:_EV_CKV], gq_ref[...]).astype(BF16)
    q = _dot(cq, wuq_ref[...])
    qpe = [_rope_chunk(q[:, _UQ_ROPE + LANES * j:_UQ_ROPE + LANES * (j + 1)], c32, s32, MLA_ROPE // 2) * q_scale
           for j in range(MLA_HEADS * MLA_ROPE // LANES)]
    heads_per_chunk = LANES // MLA_ROPE
    for hd in range(MLA_HEADS):
        q_lat = _dot(q[:, LANES * hd:LANES * (hd + 1)].astype(BF16), wuk_ref[hd]) * q_scale
        qmla_ref[:, MLA_KW * hd:MLA_KW * hd + LANES] = q_lat.astype(BF16)
        chunk = qpe[hd // heads_per_chunk]
        shift = (hd % heads_per_chunk) * MLA_ROPE
        if shift:
            chunk = pltpu.roll(chunk, LANES - shift, 1)
        qmla_ref[:, MLA_KW * hd + LANES:MLA_KW * (hd + 1)] = jnp.where(lane < MLA_ROPE, chunk, 0.0).astype(BF16)

    ckv = _rms(y[:, _EV_CKV:_EV_DQ], gkv_ref[...])
    kpe = _rope_chunk(y[:, _EV_KPE:_EV_END], c32, s32, MLA_ROPE // 2)
    mrows_ref[:, 0:KV_LORA] = ckv
    mrows_ref[:, KV_LORA:MLA_KUSED] = kpe[:, 0:MLA_ROPE]
    kmla_ref[:, 0:LANES] = ckv.astype(BF16)
    kmla_ref[:, LANES:MLA_KW] = kpe.astype(BF16)

    d_scale = DIFF_SCALE * LOG2E
    for j in range(4):
        dq = _rope_chunk(y[:, _EV_DQ + LANES * j:_EV_DQ + LANES * (j + 1)], c64, s64, DIFF_HD // 2)
        dq_ref[:, LANES * j:LANES * (j + 1)] = (dq * d_scale).astype(BF16)
    for g in range(DIFF_KV_HEADS):
        dk = _rope_chunk(y[:, _EV_DK + LANES * g:_EV_DK + LANES * (g + 1)], c64, s64, DIFF_HD // 2)
        dv = y[:, _EV_DV + LANES * g:_EV_DV + LANES * (g + 1)]
        base = 2 * LANES * g
        drows_ref[:, base:base + LANES] = dk
        drows_ref[:, base + LANES:base + 2 * LANES] = dv
        dkv_ref[:, base:base + LANES] = dk.astype(BF16)
        dkv_ref[:, base + LANES:base + 2 * LANES] = dv.astype(BF16)


def _prep_even_weights(w_in, w_uq, w_uk, w_uv, w_out):
    o = np.cumsum((0,) + EVEN_SPLITS)
    c_q, c_kv, k_pe, dq, dk, dv = (w_in[:, o[j]:o[j + 1]] for j in range(6))
    d = w_in.shape[0]
    w_in_r = jnp.concatenate([c_q, c_kv, dq, dk, dv, k_pe, jnp.zeros((d, LANES - MLA_ROPE), w_in.dtype)], 1)
    nope = jnp.concatenate([w_uq[:, :, :MLA_NOPE], jnp.zeros((Q_LORA, MLA_HEADS, LANES - MLA_NOPE), w_uq.dtype)], -1)
    w_uq_r = jnp.concatenate([nope.reshape(Q_LORA, MLA_HEADS * LANES),
                              w_uq[:, :, MLA_NOPE:].reshape(Q_LORA, MLA_HEADS * MLA_ROPE)], 1)
    wk = jnp.transpose(w_uk, (1, 2, 0))
    wk = jnp.concatenate([wk, jnp.zeros((MLA_HEADS, LANES - MLA_NOPE, KV_LORA), wk.dtype)], 1)
    eye = jnp.eye(MLA_HEADS, dtype=w_uv.dtype)
    wv = jnp.einsum('chd,hg->hcgd', w_uv, eye).reshape(MLA_HEADS * KV_LORA, MLA_HEADS * MLA_V)
    n_mla = MLA_HEADS * MLA_V
    return dict(w_in=w_in_r.astype(BF16), w_uq=w_uq_r.astype(BF16), w_uk=wk.astype(BF16), w_uv=wv.astype(BF16),
                w_out_mla=w_out[:n_mla].astype(BF16), w_out_dif=w_out[n_mla:].astype(BF16))


def _even_in(tok, x, mod, tabs, w, g_q, g_kv):
    c64, s64, c32, s32 = tabs
    n = tok.n
    outs = (SDS((n, MLA_HEADS * MLA_KW), BF16), SDS((n, MLA_KW), BF16), SDS((n, MLA_KUSED), F32),
            SDS((n, 4 * LANES), BF16), SDS((n, 4 * LANES), F32), SDS((n, 4 * LANES), BF16))
    return pl.pallas_call(
        _even_in_kernel,
        out_shape=outs,
        grid=tok.grid,
        in_specs=[tok.rows(D_MODEL), tok.mod(0), tok.mod(1), tok.pos(), tok.pos(), tok.pos(), tok.pos(),
                  _const_spec(w['w_in'].shape), _const_spec((1, Q_LORA)), _const_spec((1, KV_LORA)),
                  _const_spec(w['w_uq'].shape), _const_spec(w['w_uk'].shape)],
        out_specs=[tok.rows(MLA_HEADS * MLA_KW), tok.rows(MLA_KW), tok.rows(MLA_KUSED),
                   tok.rows(4 * LANES), tok.rows(4 * LANES), tok.rows(4 * LANES)],
        compiler_params=_params(("arbitrary",)),
        name="even_in",
    )(x, mod, mod, c64, s64, c32, s32, w['w_in'], g_q.reshape(1, Q_LORA), g_kv.reshape(1, KV_LORA),
      w['w_uq'], w['w_uk'])


def _even_out_kernel(x_ref, g_ref, lat_ref, odif_ref, wuv_ref, wo1_ref, wo2_ref, lng_ref, lnb_ref, o_ref, *, alpha):
    o_mla = _dot(lat_ref[...], wuv_ref[...]).astype(BF16)
    y = _dot(o_mla, wo1_ref[...]) + _dot(odif_ref[...], wo2_ref[...])
    z = alpha * x_ref[...] + (1.0 + g_ref[...]) * y
    o_ref[...] = _layer_norm(z, lng_ref[...], lnb_ref[...])


def _even_out(tok, x, mod, lat, odif, w, ln_g, ln_b, alpha):
    return pl.pallas_call(
        functools.partial(_even_out_kernel, alpha=alpha),
        out_shape=SDS((tok.n, D_MODEL), F32),
        grid=tok.grid,
        in_specs=[tok.rows(D_MODEL), tok.mod(2), tok.rows(MLA_HEADS * KV_LORA), tok.rows(4 * LANES),
                  _const_spec(w['w_uv'].shape), _const_spec(w['w_out_mla'].shape), _const_spec(w['w_out_dif'].shape),
                  _const_spec((1, D_MODEL)), _const_spec((1, D_MODEL))],
        out_specs=tok.rows(D_MODEL),
        compiler_params=_params(("arbitrary",)),
        name="even_out",
    )(x, mod, lat, odif, w['w_uv'], w['w_out_mla'], w['w_out_dif'], ln_g.reshape(1, -1), ln_b.reshape(1, -1))


def _odd_out_kernel(x_ref, g_ref, o_in_ref, wo_ref, lng_ref, lnb_ref, o_ref, *, alpha):
    y = _dot(o_in_ref[...], wo_ref[...])
    z = alpha * x_ref[...] + (1.0 + g_ref[...]) * y
    o_ref[...] = _layer_norm(z, lng_ref[...], lnb_ref[...])


def _odd_out(tok, x, mod, o_fox, w_out, ln_g, ln_b, alpha):
    return pl.pallas_call(
        functools.partial(_odd_out_kernel, alpha=alpha),
        out_shape=SDS((tok.n, D_MODEL), F32),
        grid=tok.grid,
        in_specs=[tok.rows(D_MODEL), tok.mod(2), tok.rows(FOX_HEADS * FOX_HD), _const_spec(w_out.shape),
                  _const_spec((1, D_MODEL)), _const_spec((1, D_MODEL))],
        out_specs=tok.rows(D_MODEL),
        compiler_params=_params(("arbitrary",)),
        name="odd_out",
    )(x, mod, o_fox, w_out, ln_g.reshape(1, -1), ln_b.reshape(1, -1))


MLP_FF_CHUNK = 1024


def _mlp_kernel(x_ref, sh_ref, sc_ref, g_ref, wup_ref, wdn_ref, lng_ref, lnb_ref, o_ref, *, alpha):
    x = x_ref[...]
    h = (x * (1.0 + sc_ref[...]) + sh_ref[...]).astype(BF16)
    y = jnp.zeros(x.shape, F32)
    for f in range(D_FF // MLP_FF_CHUNK):
        cols = slice(f * MLP_FF_CHUNK, (f + 1) * MLP_FF_CHUNK)
        u = jnp.maximum(_dot(h, wup_ref[:, cols]), 0.0)
        y = y + _dot((u * u).astype(BF16), wdn_ref[cols, :])
    z = alpha * x + (1.0 + g_ref[...]) * y
    o_ref[...] = _layer_norm(z, lng_ref[...], lnb_ref[...])


def _mlp(tok, x, mod, w_up, w_down, ln_g, ln_b, alpha):
    return pl.pallas_call(
        functools.partial(_mlp_kernel, alpha=alpha),
        out_shape=SDS((tok.n, D_MODEL), F32),
        grid=tok.grid,
        in_specs=[tok.rows(D_MODEL), tok.mod(3), tok.mod(4), tok.mod(5), _const_spec(w_up.shape),
                  _const_spec(w_down.shape), _const_spec((1, D_MODEL)), _const_spec((1, D_MODEL))],
        out_specs=tok.rows(D_MODEL),
        compiler_params=_params(("arbitrary",)),
        name="mlp",
    )(x, mod, mod, mod, w_up, w_down, ln_g.reshape(1, -1), ln_b.reshape(1, -1))


_OD_KV = FOX_HEADS * FOX_HD
_OD_F = _OD_KV + 2 * FOX_KV_HEADS * FOX_HD
_OD_END = _OD_F + LANES


def _odd_in_kernel(x_ref, sh_ref, sc_ref, win_ref, bf_ref, q_ref, kvrows_ref, kvb_ref, logf_ref):
    h = (x_ref[...] * (1.0 + sc_ref[...]) + sh_ref[...]).astype(BF16)
    y = _dot(h, win_ref[...])
    lane = lax.broadcasted_iota(I32, (y.shape[0], LANES), 1)
    scale = FOX_SCALE * LOG2E
    for hd in range(FOX_HEADS):
        chunk = y[:, LANES * (hd // 2):LANES * (hd // 2 + 1)]
        if hd % 2:
            chunk = pltpu.roll(chunk, FOX_HD, 1)
        q_ref[:, LANES * hd:LANES * (hd + 1)] = jnp.where(lane < FOX_HD, chunk * scale, 0.0).astype(BF16)
    kv = y[:, _OD_KV:_OD_F]
    kvrows_ref[...] = kv
    kvb_ref[...] = kv.astype(BF16)
    z = y[:, _OD_F:_OD_F + FOX_HEADS] + bf_ref[...]
    logf_ref[...] = jnp.minimum(z, 0.0) - jnp.log1p(jnp.exp(-jnp.abs(z)))


def _prep_odd_weights(w_in, w_out):
    o = np.cumsum((0,) + ODD_SPLITS)
    q, k, v, f = (w_in[:, o[j]:o[j + 1]] for j in range(4))
    d = w_in.shape[0]
    k = k.reshape(d, FOX_KV_HEADS, FOX_HD)
    v = v.reshape(d, FOX_KV_HEADS, FOX_HD)
    kv = jnp.concatenate([k, v], -1).reshape(d, 2 * FOX_KV_HEADS * FOX_HD)
    w_in_r = jnp.concatenate([q, kv, f, jnp.zeros((d, LANES - FOX_HEADS), w_in.dtype)], 1)
    return dict(w_in=w_in_r.astype(BF16), w_out=w_out.astype(BF16))


def _odd_in(tok, x, mod, w, b_f):
    n = tok.n
    outs = (SDS((n, FOX_HEADS * LANES), BF16), SDS((n, 4 * LANES), F32), SDS((n, 4 * LANES), BF16),
            SDS((n, FOX_HEADS), F32))
    return pl.pallas_call(
        _odd_in_kernel,
        out_shape=outs,
        grid=tok.grid,
        in_specs=[tok.rows(D_MODEL), tok.mod(0), tok.mod(1), _const_spec(w['w_in'].shape), _const_spec((1, FOX_HEADS))],
        out_specs=[tok.rows(FOX_HEADS * LANES), tok.rows(4 * LANES), tok.rows(4 * LANES), tok.rows(FOX_HEADS)],
        compiler_params=_params(("arbitrary",)),
        name="odd_in",
    )(x, mod, mod, w['w_in'], b_f.reshape(1, FOX_HEADS))


PREFIX_BLOCK = 256


def _fox_lane_matrices():
    e = np.zeros((3, FOX_HEADS, LANES), np.float32)
    for part, base in enumerate((FOX_HI, FOX_MID, FOX_LO)):
        for hd in range(FOX_HEADS):
            e[part, hd, base + hd] = 1.0
    return jnp.asarray(e, BF16)


def _ones_lanes(rows):
    lane = lax.broadcasted_iota(I32, (rows, LANES), 1)
    return jnp.where((lane >= FOX_ONE) & (lane < FOX_ONE + 3), 1.0, 0.0)


def _place_key_terms(v, e_ref):
    hi, mid, lo = _split3(v)
    a = _dot(hi, e_ref[0]) + _dot(mid, e_ref[1]) + _dot(lo, e_ref[2]) + _ones_lanes(v.shape[0])
    return a.astype(BF16)


def _query_extras(col, head, rows):
    lane = lax.broadcasted_iota(I32, (rows, LANES), 1)
    hi = col.astype(BF16).astype(F32)
    r1 = col - hi
    mid = r1.astype(BF16).astype(F32)
    lo = r1 - mid
    sel = jnp.where((lane == FOX_HI + head) | (lane == FOX_MID + head) | (lane == FOX_LO + head), 1.0, 0.0)
    return jnp.where(lane == FOX_ONE, hi, jnp.where(lane == FOX_ONE + 1, mid, jnp.where(lane == FOX_ONE + 2, lo, sel)))


def _fox_prefix_kernel(lf_ref, e_ref, augk_ref, prow_ref):
    n = lf_ref.shape[0]
    r = lax.broadcasted_iota(I32, (PREFIX_BLOCK, PREFIX_BLOCK), 0)
    c = lax.broadcasted_iota(I32, (PREFIX_BLOCK, PREFIX_BLOCK), 1)
    tri = jnp.where(r >= c, 1.0, 0.0).astype(BF16)
    carry = jnp.zeros((1, FOX_HEADS), F32)
    for blk in range(n // PREFIX_BLOCK):
        rows = pl.ds(blk * PREFIX_BLOCK, PREFIX_BLOCK)
        hi, mid, lo = _split3(lf_ref[rows, :])
        p = _dot(tri, hi) + _dot(tri, mid) + _dot(tri, lo) + carry
        carry = p[PREFIX_BLOCK - 1:PREFIX_BLOCK, :]
        p2 = p * LOG2E
        prow_ref[rows, :] = p2
        augk_ref[rows, :] = _place_key_terms(-p2, e_ref)


def _fox_prefix(logf, n_seq, seq_len, e_mat):
    n = logf.shape[0]
    return pl.pallas_call(
        _fox_prefix_kernel,
        out_shape=(SDS((n, LANES), BF16), SDS((n, FOX_HEADS), F32)),
        grid=(n_seq,),
        in_specs=[pl.BlockSpec((seq_len, FOX_HEADS), lambda b: (b, 0)), _const_spec(e_mat.shape)],
        out_specs=[pl.BlockSpec((seq_len, LANES), lambda b: (b, 0)), pl.BlockSpec((seq_len, FOX_HEADS), lambda b: (b, 0))],
        compiler_params=_params(("arbitrary",)),
        name="fox_prefix",
    )(logf, e_mat)


def _causal_sweep(i, tq, tk, step):
    n_full = (i * tq) // tk

    def body(j, carry):
        step(pl.multiple_of(j * tk, tk), None)
        return carry

    lax.fori_loop(0, n_full, body, 0)
    step(pl.multiple_of(n_full * tk, tk), n_full * tk)


def _mla_prompt_kernel(q_ref, k_ref, o_ref, m_ref, l_ref, acc_ref, *, tq, tk):
    i = pl.program_id(1)
    _init_softmax(m_ref, l_ref, acc_ref)
    q = q_ref[...]
    rows = q.shape[0]

    def step(off, k0):
        kb = k_ref[pl.ds(off, tk), :]
        s = _dot_nt(q, kb)
        if k0 is not None:
            qpos = i * tq + lax.shift_right_logical(lax.broadcasted_iota(I32, (rows, tk), 0), 3)
            kpos = k0 + lax.broadcasted_iota(I32, (rows, tk), 1)
            s = jnp.where(kpos <= qpos, s, NEG_BIG)
        _softmax_update(s, lambda p: _dot(p, kb[:, 0:KV_LORA]), m_ref, l_ref, acc_ref)

    _causal_sweep(i, tq, tk, step)
    o_ref[...] = (acc_ref[...] / l_ref[...]).astype(BF16)


def _mla_prompt(q_rows, k_rows, n_seq, seq_len, tq, tk):
    nq = seq_len // tq
    rq = tq * MLA_HEADS
    return pl.pallas_call(
        functools.partial(_mla_prompt_kernel, tq=tq, tk=tk),
        out_shape=SDS((q_rows.shape[0], KV_LORA), BF16),
        grid=(n_seq, nq),
        in_specs=[pl.BlockSpec((rq, MLA_KW), lambda b, i: (b * nq + i, 0)),
                  pl.BlockSpec((seq_len, MLA_KW), lambda b, i: (b, 0))],
        out_specs=pl.BlockSpec((rq, KV_LORA), lambda b, i: (b * nq + i, 0)),
        scratch_shapes=[pltpu.VMEM((rq, 1), F32), pltpu.VMEM((rq, 1), F32), pltpu.VMEM((rq, KV_LORA), F32)],
        compiler_params=_params(("arbitrary", "arbitrary")),
        name="mla_prompt",
    )(q_rows, k_rows)


def _diff_lambda(lq1_ref, lk1_ref, lq2_ref, lk2_ref, lam_init):
    a = jnp.exp(jnp.sum(lq1_ref[...] * lk1_ref[...], axis=1, keepdims=True))
    b = jnp.exp(jnp.sum(lq2_ref[...] * lk2_ref[...], axis=1, keepdims=True))
    return a - b + lam_init


def _diff_prompt_kernel(q_ref, kv_ref, lq1_ref, lk1_ref, lq2_ref, lk2_ref, gsub_ref, o_ref,
                        qs_ref, m_ref, l_ref, acc_ref, *, tq, tk, lam_init):
    i = pl.program_id(2)
    _init_softmax(m_ref, l_ref, acc_ref)
    lane = lax.broadcasted_iota(I32, (tq, LANES), 1)
    zero = jnp.zeros((tq, LANES), BF16)
    for hd in range(2):
        chunk = q_ref[:, LANES * hd:LANES * (hd + 1)]
        qs_ref[hd * tq:(hd + 1) * tq, :] = jnp.where(lane < DIFF_HD, chunk, zero)
        qs_ref[(2 + hd) * tq:(3 + hd) * tq, :] = jnp.where(lane >= DIFF_HD, chunk, zero)
    q = qs_ref[...]
    rows = 4 * tq

    def step(off, k0):
        kb = kv_ref[pl.ds(off, tk), 0:LANES]
        vb = kv_ref[pl.ds(off, tk), LANES:2 * LANES]
        s = _dot_nt(q, kb)
        if k0 is not None:
            qpos = i * tq + (lax.broadcasted_iota(I32, (rows, tk), 0) & (tq - 1))
            kpos = k0 + lax.broadcasted_iota(I32, (rows, tk), 1)
            s = jnp.where(kpos <= qpos, s, NEG_BIG)
        _softmax_update(s, lambda p: _dot(p, vb), m_ref, l_ref, acc_ref)

    _causal_sweep(i, tq, tk, step)
    o = acc_ref[...] / l_ref[...]
    lam = _diff_lambda(lq1_ref, lk1_ref, lq2_ref, lk2_ref, lam_init)
    for hd in range(2):
        d = o[hd * tq:(hd + 1) * tq, :] - lam * o[(2 + hd) * tq:(3 + hd) * tq, :]
        o_ref[:, LANES * hd:LANES * (hd + 1)] = (_rms(d, gsub_ref[...]) * (1.0 - lam_init)).astype(BF16)


def _lam_specs():
    return [_const_spec((1, DIFF_HD))] * 4 + [_const_spec((1, 2 * DIFF_HD))]


def _diff_prompt(dq, dkv, lam_w, g_sub, n_seq, seq_len, tq, tk, lam_init):
    nq = seq_len // tq
    return pl.pallas_call(
        functools.partial(_diff_prompt_kernel, tq=tq, tk=tk, lam_init=lam_init),
        out_shape=SDS((dq.shape[0], 4 * LANES), BF16),
        grid=(n_seq, DIFF_KV_HEADS, nq),
        in_specs=[pl.BlockSpec((tq, 2 * LANES), lambda b, g, i: (b * nq + i, g)),
                  pl.BlockSpec((seq_len, 2 * LANES), lambda b, g, i: (b, g))] + _lam_specs(),
        out_specs=pl.BlockSpec((tq, 2 * LANES), lambda b, g, i: (b * nq + i, g)),
        scratch_shapes=[pltpu.VMEM((4 * tq, LANES), BF16), pltpu.VMEM((4 * tq, 1), F32),
                        pltpu.VMEM((4 * tq, 1), F32), pltpu.VMEM((4 * tq, LANES), F32)],
        compiler_params=_params(("arbitrary", "arbitrary", "arbitrary")),
        name="diff_prompt",
    )(dq, dkv, *lam_w, g_sub)


def _fox_prompt_kernel(q_ref, kv_ref, augk_ref, prow_ref, o_ref, kaug_ref, qs_ref, m_ref, l_ref, acc_ref, *, tq, tk):
    i = pl.program_id(1)
    heads_per_kv = FOX_HEADS // FOX_KV_HEADS
    rows = heads_per_kv * tq

    @pl.when(i == 0)
    def _():
        lane_s = lax.broadcasted_iota(I32, augk_ref.shape, 1)
        a = augk_ref[...]
        for g in range(FOX_KV_HEADS):
            kaug_ref[g] = jnp.where(lane_s < FOX_HD, kv_ref[:, LANES * g:LANES * (g + 1)], a)

    prow = prow_ref[...]
    lane = lax.broadcasted_iota(I32, (tq, LANES), 1)
    for g in range(FOX_KV_HEADS):
        for r in range(heads_per_kv):
            hd = g * heads_per_kv + r
            extras = _query_extras(prow[:, hd:hd + 1], hd, tq)
            qh = q_ref[:, LANES * hd:LANES * (hd + 1)].astype(F32)
            qs_ref[r * tq:(r + 1) * tq, :] = (qh + extras).astype(BF16)
        q = qs_ref[...]
        _init_softmax(m_ref, l_ref, acc_ref)

        def step(off, k0, g=g, q=q):
            kb = kaug_ref[g, pl.ds(off, tk), :]
            vb = kv_ref[pl.ds(off, tk), LANES * g:LANES * (g + 1)]
            s = _dot_nt(q, kb)
            if k0 is not None:
                qpos = i * tq + (lax.broadcasted_iota(I32, (rows, tk), 0) & (tq - 1))
                kpos = k0 + lax.broadcasted_iota(I32, (rows, tk), 1)
                s = jnp.where(kpos <= qpos, s, NEG_BIG)
            _softmax_update(s, lambda p: _dot(p, vb), m_ref, l_ref, acc_ref)

        _causal_sweep(i, tq, tk, step)
        o = acc_ref[...] / l_ref[...]
        for pair in range(heads_per_kv // 2):
            even = pltpu.roll(o[(2 * pair) * tq:(2 * pair + 1) * tq, :], FOX_HD, 1)
            odd = o[(2 * pair + 1) * tq:(2 * pair + 2) * tq, :]
            col = (g * (heads_per_kv // 2) + pair) * LANES
            o_ref[:, col:col + LANES] = jnp.where(lane < FOX_HD, even, odd).astype(BF16)


def _fox_prompt(q, kvb, augk, prow, n_seq, seq_len, tq, tk):
    nq = seq_len // tq
    rows = (FOX_HEADS // FOX_KV_HEADS) * tq
    return pl.pallas_call(
        functools.partial(_fox_prompt_kernel, tq=tq, tk=tk),
        out_shape=SDS((q.shape[0], FOX_HEADS * FOX_HD), BF16),
        grid=(n_seq, nq),
        in_specs=[pl.BlockSpec((tq, FOX_HEADS * LANES), lambda b, i: (b * nq + i, 0)),
                  pl.BlockSpec((seq_len, 4 * LANES), lambda b, i: (b, 0)),
                  pl.BlockSpec((seq_len, LANES), lambda b, i: (b, 0)),
                  pl.BlockSpec((tq, FOX_HEADS), lambda b, i: (b * nq + i, 0))],
        out_specs=pl.BlockSpec((tq, FOX_HEADS * FOX_HD), lambda b, i: (b * nq + i, 0)),
        scratch_shapes=[pltpu.VMEM((FOX_KV_HEADS, seq_len, LANES), BF16), pltpu.VMEM((rows, LANES), BF16),
                        pltpu.VMEM((rows, 1), F32), pltpu.VMEM((rows, 1), F32), pltpu.VMEM((rows, LANES), F32)],
        compiler_params=_params(("arbitrary", "arbitrary")),
        name="fox_prompt",
    )(q, kvb, augk, prow)


def _pages_per_step(n_pages):
    return max(1, min(16, n_pages // 2))


class _PageStream:
    def __init__(self, pt_ref, cp, nch, reverse):
        self.pt = pt_ref
        self.cp = cp
        self.nch = nch
        self.reverse = reverse
        self.c = pl.program_id(0)
        self.n = pl.num_programs(0)
        self.slot = lax.rem(self.c, 2)
        self.chunk = lax.rem(self.c, nch)

    def _first_page(self, step):
        seq = step // self.nch
        ch = lax.rem(step, self.nch)
        if self.reverse:
            ch = self.nch - 1 - ch
        return seq, ch * self.cp

    def copies(self, step, slot, make):
        seq, first = self._first_page(step)
        out = []
        for p in range(self.cp):
            out.extend(make(self.pt[seq, first + p], p, slot))
        return out

    def prefetch(self, make):
        @pl.when(self.c == 0)
        def _():
            for cp_ in self.copies(self.c, self.slot, make):
                cp_.start()

        @pl.when(self.c + 1 < self.n)
        def _():
            for cp_ in self.copies(self.c + 1, 1 - self.slot, make):
                cp_.start()

    def wait(self, make):
        for cp_ in self.copies(self.c, self.slot, make):
            cp_.wait()


def _mla_decode_kernel(pt_ref, q_ref, knew_ref, cache_ref, o_ref, buf, sem, m_ref, l_ref, acc_ref, *, layer, cp, nch):
    ps = _PageStream(pt_ref, cp, nch, reverse=False)

    def make(page, p, slot):
        return [pltpu.make_async_copy(cache_ref.at[layer, page], buf.at[slot, :, pl.ds(p * PAGE_SIZE, PAGE_SIZE)],
                                      sem.at[slot])]

    ps.prefetch(make)
    q = q_ref[:, 0:MLA_KUSED]
    rows = q.shape[0]

    @pl.when(ps.chunk == 0)
    def _():
        _init_softmax(m_ref, l_ref, acc_ref)
        kn = knew_ref[...]
        s = _dot_nt(q, kn[:, 0:MLA_KUSED])
        t = lax.shift_right_logical(lax.broadcasted_iota(I32, (rows, NEW_ROWS_PAD), 0), 3)
        u = lax.broadcasted_iota(I32, (rows, NEW_ROWS_PAD), 1)
        s = jnp.where(u <= t, s, NEG_BIG)
        _softmax_update(s, lambda p: _dot(p, kn[:, 0:KV_LORA]), m_ref, l_ref, acc_ref)

    ps.wait(make)
    kt = buf[ps.slot].astype(BF16)
    s = _dot(q, kt)
    _softmax_update(s, lambda p: _dot_nt(p, kt[0:KV_LORA, :]), m_ref, l_ref, acc_ref)

    @pl.when(ps.chunk == nch - 1)
    def _():
        o_ref[...] = (acc_ref[...] / l_ref[...]).astype(BF16)


def _mla_decode(page_table, q, knew, cache_t, layer):
    n_seq, n_pages = page_table.shape
    cp = _pages_per_step(n_pages)
    nch = n_pages // cp
    rows = q.shape[1]
    return pl.pallas_call(
        functools.partial(_mla_decode_kernel, layer=layer, cp=cp, nch=nch),
        out_shape=SDS((n_seq, rows, KV_LORA), BF16),
        grid_spec=pltpu.PrefetchScalarGridSpec(
            num_scalar_prefetch=1,
            grid=(n_seq * nch,),
            in_specs=[pl.BlockSpec((None, rows, MLA_KW), lambda c, pt: (c // nch, 0, 0)),
                      pl.BlockSpec((None, NEW_ROWS_PAD, MLA_KW), lambda c, pt: (c // nch, 0, 0)),
                      pl.BlockSpec(memory_space=pl.ANY)],
            out_specs=pl.BlockSpec((None, rows, KV_LORA), lambda c, pt: (c // nch, 0, 0)),
            scratch_shapes=[pltpu.VMEM((2, MLA_KUSED, cp * PAGE_SIZE), F32), pltpu.SemaphoreType.DMA((2,)),
                            pltpu.VMEM((rows, 1), F32), pltpu.VMEM((rows, 1), F32), pltpu.VMEM((rows, KV_LORA), F32)]),
        compiler_params=_params(("arbitrary",)),
        name="mla_decode",
    )(page_table, q, knew, cache_t)


def _diff_decode_kernel(pt_ref, q_ref, knew_ref, lq1_ref, lk1_ref, lq2_ref, lk2_ref, gsub_ref, cache_ref, o_ref,
                        buf, sem, qs_ref, m_ref, l_ref, acc_ref, *, layer, cp, nch, lam_init):
    ps = _PageStream(pt_ref, cp, nch, reverse=False)

    def make(page, p, slot):
        return [pltpu.make_async_copy(cache_ref.at[layer, page, :, g, :],
                                      buf.at[slot, g, pl.ds(p * PAGE_SIZE, PAGE_SIZE), :], sem.at[slot])
                for g in range(DIFF_KV_HEADS)]

    ps.prefetch(make)
    nq = q_ref.shape[0]
    rows = 2 * nq

    @pl.when(ps.chunk == 0)
    def _():
        lane = lax.broadcasted_iota(I32, (nq, LANES), 1)
        qv = q_ref[...]
        zero = jnp.zeros((nq, LANES), BF16)
        qs_ref[0:nq, :] = jnp.where(lane < DIFF_HD, qv, zero)
        qs_ref[nq:rows, :] = jnp.where(lane >= DIFF_HD, qv, zero)
        q0 = qs_ref[...]
        t = lax.shift_right_logical(lax.broadcasted_iota(I32, (rows, NEW_ROWS_PAD), 0) & (nq - 1), 2)
        u = lax.broadcasted_iota(I32, (rows, NEW_ROWS_PAD), 1)
        for g in range(DIFF_KV_HEADS):
            _init_softmax(m_ref.at[g], l_ref.at[g], acc_ref.at[g])
            kn = knew_ref[:, 2 * LANES * g:2 * LANES * g + LANES]
            vn = knew_ref[:, 2 * LANES * g + LANES:2 * LANES * (g + 1)]
            s = jnp.where(u <= t, _dot_nt(q0, kn), NEG_BIG)
            _softmax_update(s, lambda p, vn=vn: _dot(p, vn), m_ref.at[g], l_ref.at[g], acc_ref.at[g])

    ps.wait(make)
    q = qs_ref[...]
    for g in range(DIFF_KV_HEADS):
        kvg = buf[ps.slot, g]
        kb = kvg[:, 0:LANES].astype(BF16)
        vb = kvg[:, LANES:2 * LANES].astype(BF16)
        _softmax_update(_dot_nt(q, kb), lambda p, vb=vb: _dot(p, vb), m_ref.at[g], l_ref.at[g], acc_ref.at[g])

    @pl.when(ps.chunk == nch - 1)
    def _():
        head = lax.broadcasted_iota(I32, (rows, LANES), 0) & (DIFF_HEADS - 1)
        o0 = acc_ref[0] / l_ref[0]
        o1 = acc_ref[1] / l_ref[1]
        o = jnp.where(head < DIFF_HEADS // DIFF_KV_HEADS, o0, o1)
        lam = _diff_lambda(lq1_ref, lk1_ref, lq2_ref, lk2_ref, lam_init)
        d = o[0:nq, :] - lam * o[nq:rows, :]
        o_ref[...] = (_rms(d, gsub_ref[...]) * (1.0 - lam_init)).astype(BF16)


def _diff_decode(page_table, q, knew, lam_w, g_sub, cache, layer, lam_init):
    n_seq, n_pages = page_table.shape
    cp = _pages_per_step(n_pages)
    nch = n_pages // cp
    nq = q.shape[1]
    return pl.pallas_call(
        functools.partial(_diff_decode_kernel, layer=layer, cp=cp, nch=nch, lam_init=lam_init),
        out_shape=SDS((n_seq, nq, LANES), BF16),
        grid_spec=pltpu.PrefetchScalarGridSpec(
            num_scalar_prefetch=1,
            grid=(n_seq * nch,),
            in_specs=[pl.BlockSpec((None, nq, LANES), lambda c, pt: (c // nch, 0, 0)),
                      pl.BlockSpec((None, NEW_ROWS_PAD, 4 * LANES), lambda c, pt: (c // nch, 0, 0))]
                     + _lam_specs() + [pl.BlockSpec(memory_space=pl.ANY)],
            out_specs=pl.BlockSpec((None, nq, LANES), lambda c, pt: (c // nch, 0, 0)),
            scratch_shapes=[pltpu.VMEM((2, DIFF_KV_HEADS, cp * PAGE_SIZE, 2 * LANES), F32), pltpu.SemaphoreType.DMA((2,)),
                            pltpu.VMEM((2 * nq, LANES), BF16), pltpu.VMEM((DIFF_KV_HEADS, 2 * nq, 1), F32),
                            pltpu.VMEM((DIFF_KV_HEADS, 2 * nq, 1), F32), pltpu.VMEM((DIFF_KV_HEADS, 2 * nq, LANES), F32)]),
        compiler_params=_params(("arbitrary",)),
        name="diff_decode",
    )(page_table, q, knew, *lam_w, g_sub, cache)


def _fox_decode_kernel(pt_ref, q_ref, kvnew_ref, lfnew_ref, cache_kv_ref, cache_lf_ref, o_ref,
                       bufkv, buflf, semkv, semlf, rterm_ref, carry_ref, m_ref, l_ref, acc_ref, *, layer, cp, nch, t_new):
    ps = _PageStream(pt_ref, cp, nch, reverse=True)

    def make_kv(page, p, slot):
        return [pltpu.make_async_copy(cache_kv_ref.at[layer, page, :, g, :],
                                      bufkv.at[slot, g, pl.ds(p * PAGE_SIZE, PAGE_SIZE), :], semkv.at[slot])
                for g in range(FOX_KV_HEADS)]

    def make_lf(page, p, slot):
        return [pltpu.make_async_copy(cache_lf_ref.at[layer, page], buflf.at[slot, :, pl.ds(p * PAGE_SIZE, PAGE_SIZE)],
                                      semlf.at[slot])]

    ps.prefetch(make_kv)
    ps.prefetch(make_lf)
    rows = q_ref.shape[0]
    q = q_ref[...]
    keys = cp * PAGE_SIZE
    heads_per_kv = FOX_HEADS // FOX_KV_HEADS

    @pl.when(ps.chunk == 0)
    def _():
        lf = lfnew_ref[...]
        r8 = lax.broadcasted_iota(I32, lf.shape, 0)
        cn = lf
        for sh in (1, 2, 4):
            cn = cn + jnp.where(r8 >= sh, pltpu.roll(cn, sh, 0), 0.0)
        cn = cn * LOG2E
        hh = lax.broadcasted_iota(I32, (FOX_HEADS, FOX_HEADS), 0)
        cc = lax.broadcasted_iota(I32, (FOX_HEADS, FOX_HEADS), 1)
        for t in range(t_new):
            row = jnp.broadcast_to(cn[t:t + 1, :], (FOX_HEADS, FOX_HEADS))
            rterm_ref[t * FOX_HEADS:(t + 1) * FOX_HEADS, :] = jnp.sum(jnp.where(hh == cc, row, 0.0), axis=1, keepdims=True)
        carry_ref[...] = jnp.zeros(carry_ref.shape, F32)
        eye = jnp.where(hh == cc, 1.0, 0.0).astype(BF16)
        cnt = sum(_dot_nt(eye, part) for part in _split3(cn))
        bias_new = rterm_ref[...] - jnp.concatenate([cnt] * t_new, axis=0)
        tq_ = lax.shift_right_logical(lax.broadcasted_iota(I32, (rows, NEW_ROWS_PAD), 0), 4)
        u = lax.broadcasted_iota(I32, (rows, NEW_ROWS_PAD), 1)
        for g in range(FOX_KV_HEADS):
            _init_softmax(m_ref.at[g], l_ref.at[g], acc_ref.at[g])
            kvn = kvnew_ref[:, LANES * g:LANES * (g + 1)]
            s = jnp.where(u <= tq_, _dot_nt(q, kvn) + bias_new, NEG_BIG)
            _softmax_update(s, lambda p, kvn=kvn: _dot(p, kvn), m_ref.at[g], l_ref.at[g], acc_ref.at[g])

    ps.wait(make_kv)
    ps.wait(make_lf)

    blk = PREFIX_BLOCK
    rr = lax.broadcasted_iota(I32, (blk, blk), 0)
    cc2 = lax.broadcasted_iota(I32, (blk, blk), 1)
    tri = jnp.where(rr > cc2, 1.0, 0.0).astype(BF16)
    carry = carry_ref[...]
    pieces = [None] * (keys // blk)
    for sb in reversed(range(keys // blk)):
        x = buflf[ps.slot, :, sb * blk:(sb + 1) * blk]
        hi, mid, lo = _split3(x)
        d = _dot(hi, tri) + _dot(mid, tri) + _dot(lo, tri) + carry
        carry = d[:, 0:1] + x[:, 0:1]
        pieces[sb] = d
    carry_ref[...] = carry
    dk = jnp.concatenate(pieces, axis=1) * LOG2E
    bias = jnp.concatenate([dk] * t_new, axis=0) + rterm_ref[...]

    for g in range(FOX_KV_HEADS):
        kvg = bufkv[ps.slot, g].astype(BF16)
        s = _dot_nt(q, kvg) + bias
        _softmax_update(s, lambda p, kvg=kvg: _dot(p, kvg), m_ref.at[g], l_ref.at[g], acc_ref.at[g])

    @pl.when(ps.chunk == nch - 1)
    def _():
        head = lax.broadcasted_iota(I32, (rows, LANES), 0) & (FOX_HEADS - 1)
        kvh = lax.shift_right_logical(head, 2)
        o = acc_ref[0] / l_ref[0]
        for g in range(1, FOX_KV_HEADS):
            o = jnp.where(kvh == g, acc_ref[g] / l_ref[g], o)
        o_ref[...] = o.astype(BF16)


def _fox_decode(page_table, q, kvnew, lfnew, cache_kv, cache_lf_t, layer, t_new):
    n_seq, n_pages = page_table.shape
    cp = _pages_per_step(n_pages)
    nch = n_pages // cp
    rows = q.shape[1]
    keys = cp * PAGE_SIZE
    nkv = FOX_KV_HEADS
    return pl.pallas_call(
        functools.partial(_fox_decode_kernel, layer=layer, cp=cp, nch=nch, t_new=t_new),
        out_shape=SDS((n_seq, rows, LANES), BF16),
        grid_spec=pltpu.PrefetchScalarGridSpec(
            num_scalar_prefetch=1,
            grid=(n_seq * nch,),
            in_specs=[pl.BlockSpec((None, rows, LANES), lambda c, pt: (c // nch, 0, 0)),
                      pl.BlockSpec((None, NEW_ROWS_PAD, 4 * LANES), lambda c, pt: (c // nch, 0, 0)),
                      pl.BlockSpec((None, NEW_ROWS_PAD, FOX_HEADS), lambda c, pt: (c // nch, 0, 0)),
                      pl.BlockSpec(memory_space=pl.ANY), pl.BlockSpec(memory_space=pl.ANY)],
            out_specs=pl.BlockSpec((None, rows, LANES), lambda c, pt: (c // nch, 0, 0)),
            scratch_shapes=[pltpu.VMEM((2, nkv, keys, LANES), F32), pltpu.VMEM((2, FOX_HEADS, keys), F32),
                            pltpu.SemaphoreType.DMA((2,)), pltpu.SemaphoreType.DMA((2,)),
                            pltpu.VMEM((rows, 1), F32), pltpu.VMEM((FOX_HEADS, 1), F32),
                            pltpu.VMEM((nkv, rows, 1), F32), pltpu.VMEM((nkv, rows, 1), F32),
                            pltpu.VMEM((nkv, rows, LANES), F32)]),
        compiler_params=_params(("arbitrary",)),
        name="fox_decode",
    )(page_table, q, kvnew, lfnew, cache_kv, cache_lf_t)


def _pad_new_rows(a, n_seq, t_new):
    a = a.reshape(n_seq, t_new, a.shape[-1])
    return jnp.pad(a, ((0, 0), (0, NEW_ROWS_PAD - t_new), (0, 0)))


def _pick_tile(n, pref):
    t = min(pref, n)
    while n % t:
        t //= 2
    return t


def _trunk(x, mods, tok, tabs, n_seq, seq_len, weights, depth, paged):
    alpha = (2 * depth) ** 0.25
    rows_mla, rows_diff, rows_kv, rows_lf = [], [], [], []
    e_mat = _fox_lane_matrices()
    tq = _pick_tile(seq_len, 256)
    for l in range(depth):
        mod = mods[l]
        i = l // 2
        if l % 2 == 0:
            w = weights['even'][i]
            lam_init = 0.8 - 0.6 * math.exp(-0.3 * l)
            qmla, kmla, mrows, dq, drows, dkv = _even_in(tok, x, mod, tabs, w, weights['mla_g_q'][i], weights['mla_g_kv'][i])
            lam_w = [weights[k][i].reshape(1, DIFF_HD) for k in ('diff_lam_q1', 'diff_lam_k1', 'diff_lam_q2', 'diff_lam_k2')]
            g_sub = weights['diff_g_sub'][i].reshape(1, 2 * DIFF_HD)
            if paged is None:
                tq_mla = _pick_tile(seq_len, 128)
                lat = _mla_prompt(qmla.reshape(-1, MLA_KW), kmla, n_seq, seq_len, tq_mla, tq)
                lat = lat.reshape(tok.n, MLA_HEADS * KV_LORA)
                odif = _diff_prompt(dq, dkv, lam_w, g_sub, n_seq, seq_len, tq, tq, lam_init)
            else:
                page_table, cache_mla_t, cache_diff = paged[0], paged[1], paged[2]
                lat = _mla_decode(page_table, qmla.reshape(n_seq, seq_len * MLA_HEADS, MLA_KW),
                                  _pad_new_rows(kmla, n_seq, seq_len), cache_mla_t, i)
                lat = lat.reshape(tok.n, MLA_HEADS * KV_LORA)
                odif = _diff_decode(page_table, dq.reshape(n_seq, seq_len * DIFF_HEADS, LANES),
                                    _pad_new_rows(dkv, n_seq, seq_len), lam_w, g_sub, cache_diff, i, lam_init)
                odif = odif.reshape(tok.n, DIFF_HEADS * LANES)
            x = _even_out(tok, x, mod, lat, odif, w, weights['ln_g'][l, 0], weights['ln_b'][l, 0], alpha)
            rows_mla.append(mrows)
            rows_diff.append(drows)
        else:
            w = weights['odd'][i]
            q, kvrows, kvb, logf = _odd_in(tok, x, mod, w, weights['fox_b_f'][i])
            if paged is None:
                augk, prow = _fox_prefix(logf, n_seq, seq_len, e_mat)
                o_fox = _fox_prompt(q, kvb, augk, prow, n_seq, seq_len, tq, tq)
            else:
                page_table, cache_kv, cache_lf_t = paged[0], paged[3], paged[4]
                o = _fox_decode(page_table, q.reshape(n_seq, seq_len * FOX_HEADS, LANES),
                                _pad_new_rows(kvb, n_seq, seq_len), _pad_new_rows(logf, n_seq, seq_len),
                                cache_kv, cache_lf_t, i, seq_len)
                o_fox = o[:, :, FOX_HD:].reshape(tok.n, FOX_HEADS * FOX_HD)
            x = _odd_out(tok, x, mod, o_fox, w['w_out'], weights['ln_g'][l, 0], weights['ln_b'][l, 0], alpha)
            rows_kv.append(kvrows)
            rows_lf.append(logf)
        x = _mlp(tok, x, mod, weights['mlp_w_up'][l], weights['mlp_w_down'][l],
                 weights['ln_g'][l, 1], weights['ln_b'][l, 1], alpha)
    return x, rows_mla, rows_diff, rows_kv, rows_lf


def kernel(x_prompt, x_sample, cache_mla, cache_diff, cache_fox_kv, cache_fox_logf, page_table, c_prompt, c_sample, even_w_in, mla_g_q, mla_w_uq, mla_g_kv, mla_w_uk, mla_w_uv, diff_lam_q1, diff_lam_k1, diff_lam_q2, diff_lam_k2, diff_g_sub, even_w_out, fox_w_in, fox_b_f, fox_w_out, ada_w, ada_b, ln_g, ln_b, mlp_w_up, mlp_w_down):
    depth = ada_w.shape[0]
    bp, sp, d = x_prompt.shape
    bs, ts, _ = x_sample.shape
    n_pages = page_table.shape[1]
    past_len = n_pages * PAGE_SIZE
    assert d == D_MODEL and ts <= NEW_ROWS_PAD and cache_mla.shape[2] == PAGE_SIZE

    weights = dict(
        even=[_prep_even_weights(even_w_in[i], mla_w_uq[i], mla_w_uk[i], mla_w_uv[i], even_w_out[i])
              for i in range(even_w_in.shape[0])],
        odd=[_prep_odd_weights(fox_w_in[i], fox_w_out[i]) for i in range(fox_w_in.shape[0])],
        mla_g_q=mla_g_q, mla_g_kv=mla_g_kv, diff_lam_q1=diff_lam_q1, diff_lam_k1=diff_lam_k1,
        diff_lam_q2=diff_lam_q2, diff_lam_k2=diff_lam_k2, diff_g_sub=diff_g_sub, fox_b_f=fox_b_f,
        ln_g=ln_g, ln_b=ln_b, mlp_w_up=mlp_w_up.astype(BF16), mlp_w_down=mlp_w_down.astype(BF16))

    mod = _adaln(jnp.concatenate([c_prompt, c_sample], 0), ada_w, ada_b)
    mods_p = [mod[l, :bp].reshape(bp, 1, 6 * d) for l in range(depth)]
    mods_s = [jnp.repeat(mod[l, bp:], ts, axis=0).reshape(1, bs * ts, 6 * d) for l in range(depth)]

    tm_p = _pick_tile(sp, 512)
    tok_p = _Tokens(bp * sp, tm_p, sp // tm_p, per_token=False)
    pos_p = jnp.arange(sp)
    tabs_p = _rope_tables(pos_p, DIFF_HD // 2) + _rope_tables(pos_p, MLA_ROPE // 2)
    y_p, mla_p, diff_p, kv_p, lf_p = _trunk(x_prompt.reshape(bp * sp, d), mods_p, tok_p, tabs_p, bp, sp,
                                            weights, depth, None)

    n_s = bs * ts
    tm_s = _pick_tile(n_s, 512)
    tok_s = _Tokens(n_s, tm_s, 1, per_token=True)
    pos_s = jnp.tile(past_len + jnp.arange(ts), bs)
    tabs_s = _rope_tables(pos_s, DIFF_HD // 2) + _rope_tables(pos_s, MLA_ROPE // 2)
    paged = (page_table, jnp.swapaxes(cache_mla, 2, 3), cache_diff, cache_fox_kv, jnp.swapaxes(cache_fox_logf, 2, 3))
    y_s, mla_s, diff_s, kv_s, lf_s = _trunk(x_sample.reshape(n_s, d), mods_s, tok_s, tabs_s, bs, ts,
                                            weights, depth, paged)

    def stack(rows, lead, tail):
        return jnp.stack([r.reshape(lead + tail) for r in rows])

    return (y_p.reshape(bp, sp, d), y_s.reshape(bs, ts, d),
            stack(mla_p, (bp, sp), (MLA_KUSED,)), stack(mla_s, (bs, ts), (MLA_KUSED,)),
            stack(diff_p, (bp, sp), (DIFF_KV_HEADS, 4 * DIFF_HD)), stack(diff_s, (bs, ts), (DIFF_KV_HEADS, 4 * DIFF_HD)),
            stack(kv_p, (bp, sp), (FOX_KV_HEADS, 2 * FOX_HD)), stack(kv_s, (bs, ts), (FOX_KV_HEADS, 2 * FOX_HD)),
            stack(lf_p, (bp, sp), (FOX_HEADS,)), stack(lf_s, (bs, ts), (FOX_HEADS,)))
```

```python
import functools
import math

import numpy as np
import jax
import jax.numpy as jnp
from jax import lax
from jax.experimental import pallas as pl
from jax.experimental.pallas import tpu as pltpu

F32, BF16, I32 = jnp.float32, jnp.bfloat16, jnp.int32
SDS = jax.ShapeDtypeStruct

D_MODEL = 1024
PAGE_SIZE = 128
MLA_HEADS, MLA_NOPE, MLA_ROPE, MLA_V = 8, 64, 32, 64
Q_LORA, KV_LORA = 256, 128
DIFF_HEADS, DIFF_KV_HEADS, DIFF_HD = 4, 2, 64
FOX_HEADS, FOX_KV_HEADS, FOX_HD = 16, 4, 64
D_FF = 4 * D_MODEL
ROPE_THETA = 10000.0
RMS_EPS = 1e-6
LN_EPS = 1e-5
MLA_SCALE = (MLA_NOPE + MLA_ROPE) ** -0.5
DIFF_SCALE = DIFF_HD ** -0.5
FOX_SCALE = FOX_HD ** -0.5
EVEN_SPLITS = (Q_LORA, KV_LORA, MLA_ROPE, DIFF_HEADS * 2 * DIFF_HD, DIFF_KV_HEADS * 2 * DIFF_HD,
               DIFF_KV_HEADS * 2 * DIFF_HD)
ODD_SPLITS = (FOX_HEADS * FOX_HD, FOX_KV_HEADS * FOX_HD, FOX_KV_HEADS * FOX_HD, FOX_HEADS)

LANES = 128
SUBLANES = 8
VMEM_LIMIT_BYTES = 56 * 1024 * 1024

LOG2E = 1.4426950408889634
NEG_BIG = -1e30
MLA_KW = 256
MLA_KUSED = KV_LORA + MLA_ROPE
FOX_HI, FOX_MID, FOX_LO, FOX_ONE = 64, 80, 96, 112
NEW_ROWS_PAD = 8
ATTN_TQ = 512
MLA_DECODE_PAGES = 64
KV_DECODE_PAGES = 32
TOKEN_TILE = 1024
ATTN_WIDE = 512


def _params(sem):
    return pltpu.CompilerParams(dimension_semantics=sem, vmem_limit_bytes=VMEM_LIMIT_BYTES)


def _const_spec(shape):
    nd = len(shape)
    return pl.BlockSpec(shape, lambda *_: (0,) * nd, pipeline_mode=pl.Buffered(1))


def _rms(x, g):
    return x * lax.rsqrt(jnp.mean(x * x, axis=-1, keepdims=True) + RMS_EPS) * g


def _layer_norm(x, g, b):
    mu = jnp.mean(x, axis=-1, keepdims=True)
    xc = x - mu
    var = jnp.mean(xc * xc, axis=-1, keepdims=True)
    return xc * lax.rsqrt(var + LN_EPS) * g + b


def _rope_chunk(x, cos, sin_signed, half):
    fwd = pltpu.roll(x, LANES - half, 1)
    bwd = pltpu.roll(x, half, 1)
    lane = lax.broadcasted_iota(I32, x.shape, 1)
    swapped = jnp.where((lane & half) == 0, fwd, bwd)
    return x * cos + swapped * sin_signed


def _split3(x):
    hi = x.astype(BF16)
    r1 = x - hi.astype(F32)
    mid = r1.astype(BF16)
    lo = (r1 - mid.astype(F32)).astype(BF16)
    return hi, mid, lo


def _dot(a, b):
    return jnp.dot(a, b, preferred_element_type=F32)


def _dot_nt(a, b):
    return lax.dot_general(a, b, (((1,), (1,)), ((), ())), preferred_element_type=F32)


def _lanes_to(x, n):
    if n % LANES == 0:
        return jnp.tile(x, (1, n // LANES)) if n > LANES else x
    return x[:, 0:n]


def _softmax_update(s, pv, m_ref, l_ref, acc_ref, row_term=None):
    m_prev = m_ref[...]
    m_cur = jnp.max(s, axis=1, keepdims=True)
    if row_term is not None:
        m_cur = m_cur + row_term
    m_new = jnp.maximum(m_prev, m_cur)
    alpha = jnp.exp2(m_prev - m_new)
    shift = m_new if row_term is None else m_new - row_term
    p = jnp.exp2(s - _lanes_to(shift, s.shape[1]))
    l_ref[...] = alpha * l_ref[...] + jnp.sum(p, axis=1, keepdims=True)
    acc_ref[...] = alpha * acc_ref[...] + pv(p.astype(BF16))
    m_ref[...] = m_new


def _init_softmax(m_ref, l_ref, acc_ref):
    m_ref[...] = jnp.full(m_ref.shape, NEG_BIG, F32)
    l_ref[...] = jnp.zeros(l_ref.shape, F32)
    acc_ref[...] = jnp.zeros(acc_ref.shape, F32)


def _adaln_kernel(c_ref, w_ref, b_ref, o_ref):
    c = c_ref[...]
    s = (c * jax.nn.sigmoid(c)).astype(BF16)
    o_ref[...] = _dot(s, w_ref[...].astype(BF16)) + b_ref[...]


def _adaln(c_all, ada_w, ada_b):
    n_layers, d, d6 = ada_w.shape
    bc = c_all.shape[0]
    tn = d6 // 4
    return pl.pallas_call(
        _adaln_kernel,
        out_shape=SDS((n_layers, bc, d6), F32),
        grid=(n_layers, d6 // tn),
        in_specs=[pl.BlockSpec((bc, d), lambda l, j: (0, 0)),
                  pl.BlockSpec((None, d, tn), lambda l, j: (l, 0, j)),
                  pl.BlockSpec((None, 1, tn), lambda l, j: (l, 0, j))],
        out_specs=pl.BlockSpec((None, bc, tn), lambda l, j: (l, 0, j)),
        compiler_params=_params(("arbitrary", "arbitrary")),
        name="adaln",
    )(c_all, ada_w, ada_b.reshape(n_layers, 1, d6))


class _Tokens:
    def __init__(self, n_tok, tile, tiles_per_seq, per_token):
        self.n = n_tok
        self.tm = tile
        self.tps = tiles_per_seq
        self.per_token = per_token
        self.grid = (n_tok // tile,)

    def rows(self, width):
        return pl.BlockSpec((self.tm, width), lambda i: (i, 0))

    def mod(self, chunk):
        if self.per_token:
            return pl.BlockSpec((None, self.tm, D_MODEL), lambda i: (0, i, chunk))
        tps = self.tps
        return pl.BlockSpec((None, 1, D_MODEL), lambda i: (i // tps, 0, chunk))

    def pos(self):
        if self.per_token:
            return pl.BlockSpec((self.tm, LANES), lambda i: (i, 0))
        tps = self.tps
        return pl.BlockSpec((self.tm, LANES), lambda i: (i % tps, 0))


def _rope_tables(pos, half):
    inv = ROPE_THETA ** (-jnp.arange(half, dtype=F32) / half)
    ang = pos.astype(F32)[:, None] * inv[None, :]
    cos, sin = jnp.cos(ang), jnp.sin(ang)
    reps = LANES // (2 * half)
    return (jnp.tile(jnp.concatenate([cos, cos], -1), (1, reps)),
            jnp.tile(jnp.concatenate([-sin, sin], -1), (1, reps)))


_EV_CQ, _EV_CKV, _EV_DQ, _EV_DK, _EV_DV, _EV_KPE, _EV_END = 0, 256, 384, 896, 1152, 1408, 1536
_UQ_ROPE = MLA_HEADS * LANES


def _even_in_kernel(x_ref, sh_ref, sc_ref, c64_ref, s64_ref, c32_ref, s32_ref, win_ref, gq_ref, gkv_ref,
                    wuq_ref, wuk_ref, qmla_ref, kmla_ref, mrows_ref, dq_ref, drows_ref, dkv_ref):
    h = (x_ref[...] * (1.0 + sc_ref[...]) + sh_ref[...]).astype(BF16)
    y = _dot(h, win_ref[...])
    c64, s64, c32, s32 = c64_ref[...], s64_ref[...], c32_ref[...], s32_ref[...]
    lane = lax.broadcasted_iota(I32, c64.shape, 1)
    q_scale = MLA_SCALE * LOG2E

    cq = _rms(y[:, _EV_CQ:_EV_CKV], gq_ref[...]).astype(BF16)
    q = _dot(cq, wuq_ref[...])
    qpe = [_rope_chunk(q[:, _UQ_ROPE + LANES * j:_UQ_ROPE + LANES * (j + 1)], c32, s32, MLA_ROPE // 2) * q_scale
           for j in range(MLA_HEADS * MLA_ROPE // LANES)]
    heads_per_chunk = LANES // MLA_ROPE
    for hd in range(MLA_HEADS):
        q_lat = _dot(q[:, LANES * hd:LANES * (hd + 1)].astype(BF16), wuk_ref[hd]) * q_scale
        qmla_ref[:, MLA_KW * hd:MLA_KW * hd + LANES] = q_lat.astype(BF16)
        chunk = qpe[hd // heads_per_chunk]
        shift = (hd % heads_per_chunk) * MLA_ROPE
        if shift:
            chunk = pltpu.roll(chunk, LANES - shift, 1)
        qmla_ref[:, MLA_KW * hd + LANES:MLA_KW * (hd + 1)] = jnp.where(lane < MLA_ROPE, chunk, 0.0).astype(BF16)

    ckv = _rms(y[:, _EV_CKV:_EV_DQ], gkv_ref[...])
    kpe = _rope_chunk(y[:, _EV_KPE:_EV_END], c32, s32, MLA_ROPE // 2)
    mrows_ref[:, 0:KV_LORA] = ckv
    mrows_ref[:, KV_LORA:MLA_KUSED] = kpe[:, 0:MLA_ROPE]
    kmla_ref[:, 0:LANES] = ckv.astype(BF16)
    kmla_ref[:, LANES:MLA_KW] = kpe.astype(BF16)

    d_scale = DIFF_SCALE * LOG2E
    for j in range(4):
        dq = _rope_chunk(y[:, _EV_DQ + LANES * j:_EV_DQ + LANES * (j + 1)], c64, s64, DIFF_HD // 2)
        dq_ref[:, LANES * j:LANES * (j + 1)] = (dq * d_scale).astype(BF16)
    for g in range(DIFF_KV_HEADS):
        dk = _rope_chunk(y[:, _EV_DK + LANES * g:_EV_DK + LANES * (g + 1)], c64, s64, DIFF_HD // 2)
        dv = y[:, _EV_DV + LANES * g:_EV_DV + LANES * (g + 1)]
        base = 2 * LANES * g
        drows_ref[:, base:base + LANES] = dk
        drows_ref[:, base + LANES:base + 2 * LANES] = dv
        dkv_ref[:, base:base + LANES] = dk.astype(BF16)
        dkv_ref[:, base + LANES:base + 2 * LANES] = dv.astype(BF16)


def _prep_even_weights(w_in, w_uq, w_uk, w_uv, w_out):
    o = np.cumsum((0,) + EVEN_SPLITS)
    c_q, c_kv, k_pe, dq, dk, dv = (w_in[:, o[j]:o[j + 1]] for j in range(6))
    d = w_in.shape[0]
    w_in_r = jnp.concatenate([c_q, c_kv, dq, dk, dv, k_pe, jnp.zeros((d, LANES - MLA_ROPE), w_in.dtype)], 1)
    nope = jnp.concatenate([w_uq[:, :, :MLA_NOPE], jnp.zeros((Q_LORA, MLA_HEADS, LANES - MLA_NOPE), w_uq.dtype)], -1)
    w_uq_r = jnp.concatenate([nope.reshape(Q_LORA, MLA_HEADS * LANES),
                              w_uq[:, :, MLA_NOPE:].reshape(Q_LORA, MLA_HEADS * MLA_ROPE)], 1)
    wk = jnp.transpose(w_uk, (1, 2, 0))
    wk = jnp.concatenate([wk, jnp.zeros((MLA_HEADS, LANES - MLA_NOPE, KV_LORA), wk.dtype)], 1)
    eye = jnp.eye(MLA_HEADS, dtype=w_uv.dtype)
    wv = jnp.einsum('chd,hg->hcgd', w_uv, eye).reshape(MLA_HEADS * KV_LORA, MLA_HEADS * MLA_V)
    n_mla = MLA_HEADS * MLA_V
    return dict(w_in=w_in_r.astype(BF16), w_uq=w_uq_r.astype(BF16), w_uk=wk.astype(BF16), w_uv=wv.astype(BF16),
                w_out_mla=w_out[:n_mla].astype(BF16), w_out_dif=w_out[n_mla:].astype(BF16))


def _even_in(tok, x, mod, tabs, w, g_q, g_kv):
    c64, s64, c32, s32 = tabs
    n = tok.n
    outs = (SDS((n, MLA_HEADS * MLA_KW), BF16), SDS((n, MLA_KW), BF16), SDS((n, MLA_KUSED), F32),
            SDS((n, 4 * LANES), BF16), SDS((n, 4 * LANES), F32), SDS((n, 4 * LANES), BF16))
    return pl.pallas_call(
        _even_in_kernel,
        out_shape=outs,
        grid=tok.grid,
        in_specs=[tok.rows(D_MODEL), tok.mod(0), tok.mod(1), tok.pos(), tok.pos(), tok.pos(), tok.pos(),
                  _const_spec(w['w_in'].shape), _const_spec((1, Q_LORA)), _const_spec((1, KV_LORA)),
                  _const_spec(w['w_uq'].shape), _const_spec(w['w_uk'].shape)],
        out_specs=[tok.rows(MLA_HEADS * MLA_KW), tok.rows(MLA_KW), tok.rows(MLA_KUSED),
                   tok.rows(4 * LANES), tok.rows(4 * LANES), tok.rows(4 * LANES)],
        compiler_params=_params(("arbitrary",)),
        name="even_in",
    )(x, mod, mod, c64, s64, c32, s32, w['w_in'], g_q.reshape(1, Q_LORA), g_kv.reshape(1, KV_LORA),
      w['w_uq'], w['w_uk'])


def _even_out_kernel(x_ref, g_ref, lat_ref, odif_ref, wuv_ref, wo1_ref, wo2_ref, lng_ref, lnb_ref, o_ref, *, alpha):
    o_mla = _dot(lat_ref[...], wuv_ref[...]).astype(BF16)
    y = _dot(o_mla, wo1_ref[...]) + _dot(odif_ref[...], wo2_ref[...])
    z = alpha * x_ref[...] + (1.0 + g_ref[...]) * y
    o_ref[...] = _layer_norm(z, lng_ref[...], lnb_ref[...])


def _even_out(tok, x, mod, lat, odif, w, ln_g, ln_b, alpha):
    return pl.pallas_call(
        functools.partial(_even_out_kernel, alpha=alpha),
        out_shape=SDS((tok.n, D_MODEL), F32),
        grid=tok.grid,
        in_specs=[tok.rows(D_MODEL), tok.mod(2), tok.rows(MLA_HEADS * KV_LORA), tok.rows(4 * LANES),
                  _const_spec(w['w_uv'].shape), _const_spec(w['w_out_mla'].shape), _const_spec(w['w_out_dif'].shape),
                  _const_spec((1, D_MODEL)), _const_spec((1, D_MODEL))],
        out_specs=tok.rows(D_MODEL),
        compiler_params=_params(("arbitrary",)),
        name="even_out",
    )(x, mod, lat, odif, w['w_uv'], w['w_out_mla'], w['w_out_dif'], ln_g.reshape(1, -1), ln_b.reshape(1, -1))


def _odd_out_kernel(x_ref, g_ref, o_in_ref, wo_ref, lng_ref, lnb_ref, o_ref, *, alpha):
    y = _dot(o_in_ref[...], wo_ref[...])
    z = alpha * x_ref[...] + (1.0 + g_ref[...]) * y
    o_ref[...] = _layer_norm(z, lng_ref[...], lnb_ref[...])


def _odd_out(tok, x, mod, o_fox, w_out, ln_g, ln_b, alpha):
    return pl.pallas_call(
        functools.partial(_odd_out_kernel, alpha=alpha),
        out_shape=SDS((tok.n, D_MODEL), F32),
        grid=tok.grid,
        in_specs=[tok.rows(D_MODEL), tok.mod(2), tok.rows(FOX_HEADS * FOX_HD), _const_spec(w_out.shape),
                  _const_spec((1, D_MODEL)), _const_spec((1, D_MODEL))],
        out_specs=tok.rows(D_MODEL),
        compiler_params=_params(("arbitrary",)),
        name="odd_out",
    )(x, mod, o_fox, w_out, ln_g.reshape(1, -1), ln_b.reshape(1, -1))


MLP_FF_CHUNK = 1024


def _mlp_kernel(x_ref, sh_ref, sc_ref, g_ref, wup_ref, wdn_ref, lng_ref, lnb_ref, o_ref, *, alpha):
    x = x_ref[...]
    h = (x * (1.0 + sc_ref[...]) + sh_ref[...]).astype(BF16)
    y = jnp.zeros(x.shape, F32)
    for f in range(D_FF // MLP_FF_CHUNK):
        cols = slice(f * MLP_FF_CHUNK, (f + 1) * MLP_FF_CHUNK)
        u = jnp.maximum(_dot(h, wup_ref[:, cols]), 0.0)
        y = y + _dot((u * u).astype(BF16), wdn_ref[cols, :])
    z = alpha * x + (1.0 + g_ref[...]) * y
    o_ref[...] = _layer_norm(z, lng_ref[...], lnb_ref[...])


def _mlp(tok, x, mod, w_up, w_down, ln_g, ln_b, alpha):
    return pl.pallas_call(
        functools.partial(_mlp_kernel, alpha=alpha),
        out_shape=SDS((tok.n, D_MODEL), F32),
        grid=tok.grid,
        in_specs=[tok.rows(D_MODEL), tok.mod(3), tok.mod(4), tok.mod(5), _const_spec(w_up.shape),
                  _const_spec(w_down.shape), _const_spec((1, D_MODEL)), _const_spec((1, D_MODEL))],
        out_specs=tok.rows(D_MODEL),
        compiler_params=_params(("arbitrary",)),
        name="mlp",
    )(x, mod, mod, mod, w_up, w_down, ln_g.reshape(1, -1), ln_b.reshape(1, -1))


_OD_KV = FOX_HEADS * FOX_HD
_OD_F = _OD_KV + 2 * FOX_KV_HEADS * FOX_HD
_OD_END = _OD_F + LANES


def _odd_in_kernel(x_ref, sh_ref, sc_ref, win_ref, bf_ref, q_ref, kvrows_ref, kvb_ref, logf_ref):
    h = (x_ref[...] * (1.0 + sc_ref[...]) + sh_ref[...]).astype(BF16)
    y = _dot(h, win_ref[...])
    lane = lax.broadcasted_iota(I32, (y.shape[0], LANES), 1)
    scale = FOX_SCALE * LOG2E
    for hd in range(FOX_HEADS):
        chunk = y[:, LANES * (hd // 2):LANES * (hd // 2 + 1)]
        if hd % 2:
            chunk = pltpu.roll(chunk, FOX_HD, 1)
        q_ref[:, LANES * hd:LANES * (hd + 1)] = jnp.where(lane < FOX_HD, chunk * scale, 0.0).astype(BF16)
    kv = y[:, _OD_KV:_OD_F]
    kvrows_ref[...] = kv
    kvb_ref[...] = kv.astype(BF16)
    z = y[:, _OD_F:_OD_F + FOX_HEADS] + bf_ref[...]
    logf_ref[...] = jnp.minimum(z, 0.0) - jnp.log1p(jnp.exp(-jnp.abs(z)))


def _prep_odd_weights(w_in, w_out):
    o = np.cumsum((0,) + ODD_SPLITS)
    q, k, v, f = (w_in[:, o[j]:o[j + 1]] for j in range(4))
    d = w_in.shape[0]
    k = k.reshape(d, FOX_KV_HEADS, FOX_HD)
    v = v.reshape(d, FOX_KV_HEADS, FOX_HD)
    kv = jnp.concatenate([k, v], -1).reshape(d, 2 * FOX_KV_HEADS * FOX_HD)
    w_in_r = jnp.concatenate([q, kv, f, jnp.zeros((d, LANES - FOX_HEADS), w_in.dtype)], 1)
    return dict(w_in=w_in_r.astype(BF16), w_out=w_out.astype(BF16))


def _odd_in(tok, x, mod, w, b_f):
    n = tok.n
    outs = (SDS((n, FOX_HEADS * LANES), BF16), SDS((n, 4 * LANES), F32), SDS((n, 4 * LANES), BF16),
            SDS((n, FOX_HEADS), F32))
    return pl.pallas_call(
        _odd_in_kernel,
        out_shape=outs,
        grid=tok.grid,
        in_specs=[tok.rows(D_MODEL), tok.mod(0), tok.mod(1), _const_spec(w['w_in'].shape), _const_spec((1, FOX_HEADS))],
        out_specs=[tok.rows(FOX_HEADS * LANES), tok.rows(4 * LANES), tok.rows(4 * LANES), tok.rows(FOX_HEADS)],
        compiler_params=_params(("arbitrary",)),
        name="odd_in",
    )(x, mod, mod, w['w_in'], b_f.reshape(1, FOX_HEADS))


PREFIX_BLOCK = 256


def _fox_lane_matrices():
    e = np.zeros((3, FOX_HEADS, LANES), np.float32)
    for part, base in enumerate((FOX_HI, FOX_MID, FOX_LO)):
        for hd in range(FOX_HEADS):
            e[part, hd, base + hd] = 1.0
    return jnp.asarray(e, BF16)


def _ones_lanes(rows):
    lane = lax.broadcasted_iota(I32, (rows, LANES), 1)
    return jnp.where((lane >= FOX_ONE) & (lane < FOX_ONE + 3), 1.0, 0.0)


def _place_key_terms(v, e_ref):
    hi, mid, lo = _split3(v)
    a = _dot(hi, e_ref[0]) + _dot(mid, e_ref[1]) + _dot(lo, e_ref[2]) + _ones_lanes(v.shape[0])
    return a.astype(BF16)


def _query_extras(col, head, rows):
    lane = lax.broadcasted_iota(I32, (rows, LANES), 1)
    hi = col.astype(BF16).astype(F32)
    r1 = col - hi
    mid = r1.astype(BF16).astype(F32)
    lo = r1 - mid
    sel = jnp.where((lane == FOX_HI + head) | (lane == FOX_MID + head) | (lane == FOX_LO + head), 1.0, 0.0)
    return jnp.where(lane == FOX_ONE, hi, jnp.where(lane == FOX_ONE + 1, mid, jnp.where(lane == FOX_ONE + 2, lo, sel)))


def _fox_prefix_kernel(lf_ref, e_ref, augk_ref, prow_ref):
    n = lf_ref.shape[0]
    r = lax.broadcasted_iota(I32, (PREFIX_BLOCK, PREFIX_BLOCK), 0)
    c = lax.broadcasted_iota(I32, (PREFIX_BLOCK, PREFIX_BLOCK), 1)
    tri = jnp.where(r >= c, 1.0, 0.0).astype(BF16)
    carry = jnp.zeros((1, FOX_HEADS), F32)
    for blk in range(n // PREFIX_BLOCK):
        rows = pl.ds(blk * PREFIX_BLOCK, PREFIX_BLOCK)
        hi, mid, lo = _split3(lf_ref[rows, :])
        p = _dot(tri, hi) + _dot(tri, mid) + _dot(tri, lo) + carry
        carry = p[PREFIX_BLOCK - 1:PREFIX_BLOCK, :]
        p2 = p * LOG2E
        prow_ref[rows, :] = p2
        augk_ref[rows, :] = _place_key_terms(-p2, e_ref)


def _fox_prefix(logf, n_seq, seq_len, e_mat):
    n = logf.shape[0]
    return pl.pallas_call(
        _fox_prefix_kernel,
        out_shape=(SDS((n, LANES), BF16), SDS((n, FOX_HEADS), F32)),
        grid=(n_seq,),
        in_specs=[pl.BlockSpec((seq_len, FOX_HEADS), lambda b: (b, 0)), _const_spec(e_mat.shape)],
        out_specs=[pl.BlockSpec((seq_len, LANES), lambda b: (b, 0)), pl.BlockSpec((seq_len, FOX_HEADS), lambda b: (b, 0))],
        compiler_params=_params(("arbitrary",)),
        name="fox_prefix",
    )(logf, e_mat)


def _causal_sweep(i, tq, wide, step):
    ratio = wide // tq
    n_wide = lax.div(i, ratio)
    n_rem = i - n_wide * ratio

    def wide_body(j, carry):
        step(pl.multiple_of(j * wide, wide), wide, False)
        return carry

    def rem_body(j, carry):
        step(pl.multiple_of((n_wide * ratio + j) * tq, tq), tq, False)
        return carry

    lax.fori_loop(0, n_wide, wide_body, 0)
    lax.fori_loop(0, n_rem, rem_body, 0)
    step(pl.multiple_of(i * tq, tq), tq, True)


def _diag_mask(rows, tq, token_of_row):
    r = lax.broadcasted_iota(I32, (rows, tq), 0)
    c = lax.broadcasted_iota(I32, (rows, tq), 1)
    return jnp.where(c <= token_of_row(r), 0.0, NEG_BIG)


def _mla_prompt_kernel(q_ref, k_ref, o_ref, qs_ref, m_ref, l_ref, acc_ref, *, tq, wide):
    i = pl.program_id(1)
    _init_softmax(m_ref, l_ref, acc_ref)
    for hd in range(MLA_HEADS):
        qs_ref[hd * tq:(hd + 1) * tq, :] = q_ref[:, MLA_KW * hd:MLA_KW * (hd + 1)]
    q = qs_ref[...]
    mask = jnp.tile(_diag_mask(tq, tq, lambda r: r), (MLA_HEADS, 1))

    def step(off, width, masked):
        kb = k_ref[pl.ds(off, width), :]
        s = _dot_nt(q, kb)
        if masked:
            s = s + mask
        _softmax_update(s, lambda p: _dot(p, kb[:, 0:KV_LORA]), m_ref, l_ref, acc_ref)

    _causal_sweep(i, tq, wide, step)
    o = (acc_ref[...] / l_ref[...]).astype(BF16)
    for hd in range(MLA_HEADS):
        o_ref[:, KV_LORA * hd:KV_LORA * (hd + 1)] = o[hd * tq:(hd + 1) * tq, :]


def _mla_prompt(q, k_rows, n_seq, seq_len, tq, wide):
    nq = seq_len // tq
    rq = tq * MLA_HEADS
    return pl.pallas_call(
        functools.partial(_mla_prompt_kernel, tq=tq, wide=wide),
        out_shape=SDS((q.shape[0], MLA_HEADS * KV_LORA), BF16),
        grid=(n_seq, nq),
        in_specs=[pl.BlockSpec((tq, MLA_HEADS * MLA_KW), lambda b, i: (b * nq + i, 0)),
                  pl.BlockSpec((seq_len, MLA_KW), lambda b, i: (b, 0))],
        out_specs=pl.BlockSpec((tq, MLA_HEADS * KV_LORA), lambda b, i: (b * nq + i, 0)),
        scratch_shapes=[pltpu.VMEM((rq, MLA_KW), BF16), pltpu.VMEM((rq, LANES), F32), pltpu.VMEM((rq, LANES), F32),
                        pltpu.VMEM((rq, KV_LORA), F32)],
        compiler_params=_params(("arbitrary", "arbitrary")),
        name="mla_prompt",
    )(q, k_rows)


def _diff_lambda(lq1_ref, lk1_ref, lq2_ref, lk2_ref, lam_init):
    a = jnp.exp(jnp.sum(lq1_ref[...] * lk1_ref[...], axis=1, keepdims=True))
    b = jnp.exp(jnp.sum(lq2_ref[...] * lk2_ref[...], axis=1, keepdims=True))
    return a - b + lam_init


def _diff_prompt_kernel(q_ref, kv_ref, lq1_ref, lk1_ref, lq2_ref, lk2_ref, gsub_ref, o_ref,
                        qs_ref, m_ref, l_ref, acc_ref, *, tq, wide, lam_init):
    i = pl.program_id(2)
    _init_softmax(m_ref, l_ref, acc_ref)
    lane = lax.broadcasted_iota(I32, (tq, LANES), 1)
    zero = jnp.zeros((tq, LANES), BF16)
    for hd in range(2):
        chunk = q_ref[:, LANES * hd:LANES * (hd + 1)]
        qs_ref[hd * tq:(hd + 1) * tq, :] = jnp.where(lane < DIFF_HD, chunk, zero)
        qs_ref[(2 + hd) * tq:(3 + hd) * tq, :] = jnp.where(lane >= DIFF_HD, chunk, zero)
    q = qs_ref[...]
    mask = jnp.tile(_diag_mask(tq, tq, lambda r: r), (4, 1))

    def step(off, width, masked):
        kb = kv_ref[pl.ds(off, width), 0:LANES]
        vb = kv_ref[pl.ds(off, width), LANES:2 * LANES]
        s = _dot_nt(q, kb)
        if masked:
            s = s + mask
        _softmax_update(s, lambda p: _dot(p, vb), m_ref, l_ref, acc_ref)

    _causal_sweep(i, tq, wide, step)
    o = acc_ref[...] / l_ref[...]
    lam = _diff_lambda(lq1_ref, lk1_ref, lq2_ref, lk2_ref, lam_init)
    for hd in range(2):
        d = o[hd * tq:(hd + 1) * tq, :] - lam * o[(2 + hd) * tq:(3 + hd) * tq, :]
        o_ref[:, LANES * hd:LANES * (hd + 1)] = (_rms(d, gsub_ref[...]) * (1.0 - lam_init)).astype(BF16)


def _lam_specs():
    return [_const_spec((1, DIFF_HD))] * 4 + [_const_spec((1, 2 * DIFF_HD))]


def _diff_prompt(dq, dkv, lam_w, g_sub, n_seq, seq_len, tq, wide, lam_init):
    nq = seq_len // tq
    return pl.pallas_call(
        functools.partial(_diff_prompt_kernel, tq=tq, wide=wide, lam_init=lam_init),
        out_shape=SDS((dq.shape[0], 4 * LANES), BF16),
        grid=(n_seq, DIFF_KV_HEADS, nq),
        in_specs=[pl.BlockSpec((tq, 2 * LANES), lambda b, g, i: (b * nq + i, g)),
                  pl.BlockSpec((seq_len, 2 * LANES), lambda b, g, i: (b, g))] + _lam_specs(),
        out_specs=pl.BlockSpec((tq, 2 * LANES), lambda b, g, i: (b * nq + i, g)),
        scratch_shapes=[pltpu.VMEM((4 * tq, LANES), BF16), pltpu.VMEM((4 * tq, LANES), F32),
                        pltpu.VMEM((4 * tq, LANES), F32), pltpu.VMEM((4 * tq, LANES), F32)],
        compiler_params=_params(("arbitrary", "arbitrary", "arbitrary")),
        name="diff_prompt",
    )(dq, dkv, *lam_w, g_sub)


def _fox_prompt_kernel(q_ref, kv_ref, augk_ref, prow_ref, o_ref, kaug_ref, qs_ref, m_ref, l_ref, acc_ref, *, tq, wide):
    i = pl.program_id(1)
    heads_per_kv = FOX_HEADS // FOX_KV_HEADS

    @pl.when(i == 0)
    def _():
        lane_s = lax.broadcasted_iota(I32, augk_ref.shape, 1)
        a = augk_ref[...]
        for g in range(FOX_KV_HEADS):
            kaug_ref[g] = jnp.where(lane_s < FOX_HD, kv_ref[:, LANES * g:LANES * (g + 1)], a)

    prow = prow_ref[...]
    lane = lax.broadcasted_iota(I32, (tq, LANES), 1)
    mask = jnp.tile(_diag_mask(tq, tq, lambda r: r), (heads_per_kv, 1))
    for g in range(FOX_KV_HEADS):
        for r in range(heads_per_kv):
            hd = g * heads_per_kv + r
            extras = _query_extras(prow[:, hd:hd + 1], hd, tq)
            qh = q_ref[:, LANES * hd:LANES * (hd + 1)].astype(F32)
            qs_ref[r * tq:(r + 1) * tq, :] = (qh + extras).astype(BF16)
        q = qs_ref[...]
        _init_softmax(m_ref, l_ref, acc_ref)

        def step(off, width, masked, g=g, q=q):
            kb = kaug_ref[g, pl.ds(off, width), :]
            vb = kv_ref[pl.ds(off, width), LANES * g:LANES * (g + 1)]
            s = _dot_nt(q, kb)
            if masked:
                s = s + mask
            _softmax_update(s, lambda p: _dot(p, vb), m_ref, l_ref, acc_ref)

        _causal_sweep(i, tq, wide, step)
        o = acc_ref[...] / l_ref[...]
        for pair in range(heads_per_kv // 2):
            even = pltpu.roll(o[(2 * pair) * tq:(2 * pair + 1) * tq, :], FOX_HD, 1)
            odd = o[(2 * pair + 1) * tq:(2 * pair + 2) * tq, :]
            col = (g * (heads_per_kv // 2) + pair) * LANES
            o_ref[:, col:col + LANES] = jnp.where(lane < FOX_HD, even, odd).astype(BF16)


def _fox_prompt(q, kvb, augk, prow, n_seq, seq_len, tq, wide):
    nq = seq_len // tq
    rows = (FOX_HEADS // FOX_KV_HEADS) * tq
    return pl.pallas_call(
        functools.partial(_fox_prompt_kernel, tq=tq, wide=wide),
        out_shape=SDS((q.shape[0], FOX_HEADS * FOX_HD), BF16),
        grid=(n_seq, nq),
        in_specs=[pl.BlockSpec((tq, FOX_HEADS * LANES), lambda b, i: (b * nq + i, 0)),
                  pl.BlockSpec((seq_len, 4 * LANES), lambda b, i: (b, 0)),
                  pl.BlockSpec((seq_len, LANES), lambda b, i: (b, 0)),
                  pl.BlockSpec((tq, FOX_HEADS), lambda b, i: (b * nq + i, 0))],
        out_specs=pl.BlockSpec((tq, FOX_HEADS * FOX_HD), lambda b, i: (b * nq + i, 0)),
        scratch_shapes=[pltpu.VMEM((FOX_KV_HEADS, seq_len, LANES), BF16), pltpu.VMEM((rows, LANES), BF16),
                        pltpu.VMEM((rows, LANES), F32), pltpu.VMEM((rows, LANES), F32), pltpu.VMEM((rows, LANES), F32)],
        compiler_params=_params(("arbitrary", "arbitrary")),
        name="fox_prompt",
    )(q, kvb, augk, prow)


def _pages_per_step(n_pages, pref):
    return max(1, min(pref, n_pages // 2))


class _PageStream:
    def __init__(self, pt_ref, cp, nch, reverse):
        self.pt = pt_ref
        self.cp = cp
        self.nch = nch
        self.reverse = reverse
        self.c = pl.program_id(0)
        self.n = pl.num_programs(0)
        self.slot = lax.rem(self.c, 2)
        self.chunk = lax.rem(self.c, nch)

    def _first_page(self, step):
        seq = step // self.nch
        ch = lax.rem(step, self.nch)
        if self.reverse:
            ch = self.nch - 1 - ch
        return seq, ch * self.cp

    def copies(self, step, slot, make):
        seq, first = self._first_page(step)
        out = []
        for p in range(self.cp):
            out.extend(make(self.pt[seq, first + p], p, slot))
        return out

    def prefetch(self, make):
        @pl.when(self.c == 0)
        def _():
            for cp_ in self.copies(self.c, self.slot, make):
                cp_.start()

        @pl.when(self.c + 1 < self.n)
        def _():
            for cp_ in self.copies(self.c + 1, 1 - self.slot, make):
                cp_.start()

    def wait(self, make):
        for cp_ in self.copies(self.c, self.slot, make):
            cp_.wait()


def _mla_decode_kernel(pt_ref, q_ref, knew_ref, cache_ref, o_ref, buf, sem, m_ref, l_ref, acc_ref, *, layer, cp, nch):
    ps = _PageStream(pt_ref, cp, nch, reverse=False)

    def make(page, p, slot):
        return [pltpu.make_async_copy(cache_ref.at[layer, page], buf.at[slot, :, pl.ds(p * PAGE_SIZE, PAGE_SIZE)],
                                      sem.at[slot])]

    ps.prefetch(make)
    q = q_ref[:, 0:MLA_KUSED]
    rows = q.shape[0]

    @pl.when(ps.chunk == 0)
    def _():
        _init_softmax(m_ref, l_ref, acc_ref)
        kn = knew_ref[...]
        s = _dot_nt(q, kn[:, 0:MLA_KUSED])
        t = lax.shift_right_logical(lax.broadcasted_iota(I32, (rows, NEW_ROWS_PAD), 0), 3)
        u = lax.broadcasted_iota(I32, (rows, NEW_ROWS_PAD), 1)
        s = jnp.where(u <= t, s, NEG_BIG)
        _softmax_update(s, lambda p: _dot(p, kn[:, 0:KV_LORA]), m_ref, l_ref, acc_ref)

    ps.wait(make)
    kt = buf[ps.slot].astype(BF16)
    s = _dot(q, kt)
    _softmax_update(s, lambda p: _dot_nt(p, kt[0:KV_LORA, :]), m_ref, l_ref, acc_ref)

    @pl.when(ps.chunk == nch - 1)
    def _():
        o_ref[...] = (acc_ref[...] / l_ref[...]).astype(BF16)


def _mla_decode(page_table, q, knew, cache_t, layer):
    n_seq, n_pages = page_table.shape
    cp = _pages_per_step(n_pages, MLA_DECODE_PAGES)
    nch = n_pages // cp
    rows = q.shape[1]
    return pl.pallas_call(
        functools.partial(_mla_decode_kernel, layer=layer, cp=cp, nch=nch),
        out_shape=SDS((n_seq, rows, KV_LORA), BF16),
        grid_spec=pltpu.PrefetchScalarGridSpec(
            num_scalar_prefetch=1,
            grid=(n_seq * nch,),
            in_specs=[pl.BlockSpec((None, rows, MLA_KW), lambda c, pt: (c // nch, 0, 0)),
                      pl.BlockSpec((None, NEW_ROWS_PAD, MLA_KW), lambda c, pt: (c // nch, 0, 0)),
                      pl.BlockSpec(memory_space=pl.ANY)],
            out_specs=pl.BlockSpec((None, rows, KV_LORA), lambda c, pt: (c // nch, 0, 0)),
            scratch_shapes=[pltpu.VMEM((2, MLA_KUSED, cp * PAGE_SIZE), F32), pltpu.SemaphoreType.DMA((2,)),
                            pltpu.VMEM((rows, LANES), F32), pltpu.VMEM((rows, LANES), F32), pltpu.VMEM((rows, KV_LORA), F32)]),
        compiler_params=_params(("arbitrary",)),
        name="mla_decode",
    )(page_table, q, knew, cache_t)


def _diff_decode_kernel(pt_ref, q_ref, knew_ref, lq1_ref, lk1_ref, lq2_ref, lk2_ref, gsub_ref, cache_ref, o_ref,
                        buf, sem, qs_ref, m_ref, l_ref, acc_ref, *, layer, cp, nch, lam_init):
    ps = _PageStream(pt_ref, cp, nch, reverse=False)
    per_kv = DIFF_HEADS // DIFF_KV_HEADS
    gq = per_kv * NEW_ROWS_PAD
    rows = 2 * DIFF_HEADS * NEW_ROWS_PAD

    def make(page, p, slot):
        return [pltpu.make_async_copy(cache_ref.at[layer, page, :, g, :],
                                      buf.at[slot, g, pl.ds(p * PAGE_SIZE, PAGE_SIZE), :], sem.at[slot])
                for g in range(DIFF_KV_HEADS)]

    ps.prefetch(make)

    def scores(key_of_group):
        q = qs_ref[...].astype(BF16)
        return jnp.concatenate([_dot_nt(q[2 * gq * g:2 * gq * (g + 1), :], key_of_group(g))
                                for g in range(DIFF_KV_HEADS)], axis=0)

    def values(value_of_group):
        return lambda p: jnp.concatenate([_dot(p[2 * gq * g:2 * gq * (g + 1), :], value_of_group(g))
                                          for g in range(DIFF_KV_HEADS)], axis=0)

    @pl.when(ps.chunk == 0)
    def _():
        lane = lax.broadcasted_iota(I32, (gq, LANES), 1)
        for g in range(DIFF_KV_HEADS):
            qg = q_ref[gq * g:gq * (g + 1), :].astype(F32)
            qs_ref[2 * gq * g:2 * gq * g + gq, :] = jnp.where(lane < DIFF_HD, qg, 0.0)
            qs_ref[2 * gq * g + gq:2 * gq * (g + 1), :] = jnp.where(lane >= DIFF_HD, qg, 0.0)
        _init_softmax(m_ref, l_ref, acc_ref)
        t = lax.broadcasted_iota(I32, (rows, NEW_ROWS_PAD), 0) & (NEW_ROWS_PAD - 1)
        u = lax.broadcasted_iota(I32, (rows, NEW_ROWS_PAD), 1)
        s = scores(lambda g: knew_ref[:, 2 * LANES * g:2 * LANES * g + LANES])
        s = jnp.where(u <= t, s, NEG_BIG)
        _softmax_update(s, values(lambda g: knew_ref[:, 2 * LANES * g + LANES:2 * LANES * (g + 1)]), m_ref, l_ref, acc_ref)

    ps.wait(make)
    kv = [buf[ps.slot, g] for g in range(DIFF_KV_HEADS)]
    s = scores(lambda g: kv[g][:, 0:LANES].astype(BF16))
    _softmax_update(s, values(lambda g: kv[g][:, LANES:2 * LANES].astype(BF16)), m_ref, l_ref, acc_ref)

    @pl.when(ps.chunk == nch - 1)
    def _():
        o = acc_ref[...] / l_ref[...]
        lam = _diff_lambda(lq1_ref, lk1_ref, lq2_ref, lk2_ref, lam_init)
        for g in range(DIFF_KV_HEADS):
            d = o[2 * gq * g:2 * gq * g + gq, :] - lam * o[2 * gq * g + gq:2 * gq * (g + 1), :]
            o_ref[gq * g:gq * (g + 1), :] = (_rms(d, gsub_ref[...]) * (1.0 - lam_init)).astype(BF16)


def _diff_decode(page_table, q, knew, lam_w, g_sub, cache, layer, lam_init):
    n_seq, n_pages = page_table.shape
    cp = _pages_per_step(n_pages, KV_DECODE_PAGES)
    nch = n_pages // cp
    nq = q.shape[1]
    assert nq == DIFF_HEADS * NEW_ROWS_PAD
    return pl.pallas_call(
        functools.partial(_diff_decode_kernel, layer=layer, cp=cp, nch=nch, lam_init=lam_init),
        out_shape=SDS((n_seq, nq, LANES), BF16),
        grid_spec=pltpu.PrefetchScalarGridSpec(
            num_scalar_prefetch=1,
            grid=(n_seq * nch,),
            in_specs=[pl.BlockSpec((None, nq, LANES), lambda c, pt: (c // nch, 0, 0)),
                      pl.BlockSpec((None, NEW_ROWS_PAD, 4 * LANES), lambda c, pt: (c // nch, 0, 0))]
                     + _lam_specs() + [pl.BlockSpec(memory_space=pl.ANY)],
            out_specs=pl.BlockSpec((None, nq, LANES), lambda c, pt: (c // nch, 0, 0)),
            scratch_shapes=[pltpu.VMEM((2, DIFF_KV_HEADS, cp * PAGE_SIZE, 2 * LANES), F32), pltpu.SemaphoreType.DMA((2,)),
                            pltpu.VMEM((2 * nq, LANES), F32), pltpu.VMEM((2 * nq, LANES), F32),
                            pltpu.VMEM((2 * nq, LANES), F32), pltpu.VMEM((2 * nq, LANES), F32)]),
        compiler_params=_params(("arbitrary",)),
        name="diff_decode",
    )(page_table, q, knew, *lam_w, g_sub, cache)


def _fox_decode_kernel(pt_ref, q_ref, kvnew_ref, lfnew_ref, cache_kv_ref, cache_lf_ref, o_ref,
                       bufkv, buflf, semkv, semlf, rterm_ref, carry_ref, m_ref, l_ref, acc_ref, *, layer, cp, nch):
    ps = _PageStream(pt_ref, cp, nch, reverse=True)
    per_kv = FOX_HEADS // FOX_KV_HEADS
    gq = per_kv * NEW_ROWS_PAD
    rows = FOX_HEADS * NEW_ROWS_PAD
    keys = cp * PAGE_SIZE

    def make_kv(page, p, slot):
        return [pltpu.make_async_copy(cache_kv_ref.at[layer, page, :, g, :],
                                      bufkv.at[slot, g, pl.ds(p * PAGE_SIZE, PAGE_SIZE), :], semkv.at[slot])
                for g in range(FOX_KV_HEADS)]

    def make_lf(page, p, slot):
        return [pltpu.make_async_copy(cache_lf_ref.at[layer, page], buflf.at[slot, :, pl.ds(p * PAGE_SIZE, PAGE_SIZE)],
                                      semlf.at[slot])]

    ps.prefetch(make_kv)
    ps.prefetch(make_lf)
    q = q_ref[...]

    def scores(kv_of_group):
        return jnp.concatenate([_dot_nt(q[gq * g:gq * (g + 1), :], kv_of_group(g)) for g in range(FOX_KV_HEADS)], axis=0)

    def values(kv_of_group):
        return lambda p: jnp.concatenate([_dot(p[gq * g:gq * (g + 1), :], kv_of_group(g))
                                          for g in range(FOX_KV_HEADS)], axis=0)

    def per_head_rows(x):
        return jnp.concatenate([jnp.broadcast_to(x[hd:hd + 1, :], (NEW_ROWS_PAD, x.shape[1])) for hd in range(FOX_HEADS)],
                               axis=0)

    @pl.when(ps.chunk == 0)
    def _():
        lf = lfnew_ref[...]
        r8 = lax.broadcasted_iota(I32, lf.shape, 0)
        cn = lf
        for sh in (1, 2, 4):
            cn = cn + jnp.where(r8 >= sh, pltpu.roll(cn, sh, 0), 0.0)
        cn = cn * LOG2E
        for hd in range(FOX_HEADS):
            rterm_ref[NEW_ROWS_PAD * hd:NEW_ROWS_PAD * (hd + 1), :] = jnp.broadcast_to(cn[:, hd:hd + 1],
                                                                                       (NEW_ROWS_PAD, LANES))
        carry_ref[...] = jnp.zeros(carry_ref.shape, F32)
        hh = lax.broadcasted_iota(I32, (FOX_HEADS, FOX_HEADS), 0)
        cc = lax.broadcasted_iota(I32, (FOX_HEADS, FOX_HEADS), 1)
        eye = jnp.where(hh == cc, 1.0, 0.0).astype(BF16)
        cnt = sum(_dot_nt(eye, part) for part in _split3(cn))
        bias_new = -per_head_rows(cnt)
        t = lax.broadcasted_iota(I32, (rows, NEW_ROWS_PAD), 0) & (NEW_ROWS_PAD - 1)
        u = lax.broadcasted_iota(I32, (rows, NEW_ROWS_PAD), 1)
        _init_softmax(m_ref, l_ref, acc_ref)
        kvn = lambda g: kvnew_ref[:, LANES * g:LANES * (g + 1)]
        s = jnp.where(u <= t, scores(kvn) + bias_new, NEG_BIG)
        _softmax_update(s, values(kvn), m_ref, l_ref, acc_ref, row_term=rterm_ref[...])

    ps.wait(make_kv)
    ps.wait(make_lf)

    blk = PREFIX_BLOCK
    rr = lax.broadcasted_iota(I32, (blk, blk), 0)
    cc2 = lax.broadcasted_iota(I32, (blk, blk), 1)
    tri = jnp.where(rr > cc2, 1.0, 0.0).astype(BF16)
    nblk = keys // blk
    local, total = [], []
    for sb in range(nblk):
        x = buflf[ps.slot, :, sb * blk:(sb + 1) * blk]
        hi, mid, lo = _split3(x)
        d = _dot(hi, tri) + _dot(mid, tri) + _dot(lo, tri)
        local.append(d)
        total.append(d[:, 0:1] + x[:, 0:1])
    carry = carry_ref[:, 0:1]
    pieces = [None] * nblk
    for sb in reversed(range(nblk)):
        pieces[sb] = local[sb] + carry
        carry = carry + total[sb]
    carry_ref[...] = jnp.broadcast_to(carry, carry_ref.shape)
    dk = jnp.concatenate(pieces, axis=1) * LOG2E
    bias = per_head_rows(dk)

    kvb = [bufkv[ps.slot, g].astype(BF16) for g in range(FOX_KV_HEADS)]
    s = scores(lambda g: kvb[g]) + bias
    _softmax_update(s, values(lambda g: kvb[g]), m_ref, l_ref, acc_ref, row_term=rterm_ref[...])

    @pl.when(ps.chunk == nch - 1)
    def _():
        o_ref[...] = (acc_ref[...] / l_ref[...]).astype(BF16)


def _fox_decode(page_table, q, kvnew, lfnew, cache_kv, cache_lf_t, layer):
    n_seq, n_pages = page_table.shape
    cp = _pages_per_step(n_pages, KV_DECODE_PAGES)
    nch = n_pages // cp
    rows = q.shape[1]
    assert rows == FOX_HEADS * NEW_ROWS_PAD
    keys = cp * PAGE_SIZE
    nkv = FOX_KV_HEADS
    return pl.pallas_call(
        functools.partial(_fox_decode_kernel, layer=layer, cp=cp, nch=nch),
        out_shape=SDS((n_seq, rows, LANES), BF16),
        grid_spec=pltpu.PrefetchScalarGridSpec(
            num_scalar_prefetch=1,
            grid=(n_seq * nch,),
            in_specs=[pl.BlockSpec((None, rows, LANES), lambda c, pt: (c // nch, 0, 0)),
                      pl.BlockSpec((None, NEW_ROWS_PAD, 4 * LANES), lambda c, pt: (c // nch, 0, 0)),
                      pl.BlockSpec((None, NEW_ROWS_PAD, FOX_HEADS), lambda c, pt: (c // nch, 0, 0)),
                      pl.BlockSpec(memory_space=pl.ANY), pl.BlockSpec(memory_space=pl.ANY)],
            out_specs=pl.BlockSpec((None, rows, LANES), lambda c, pt: (c // nch, 0, 0)),
            scratch_shapes=[pltpu.VMEM((2, nkv, keys, LANES), F32), pltpu.VMEM((2, FOX_HEADS, keys), F32),
                            pltpu.SemaphoreType.DMA((2,)), pltpu.SemaphoreType.DMA((2,)),
                            pltpu.VMEM((rows, LANES), F32), pltpu.VMEM((FOX_HEADS, LANES), F32),
                            pltpu.VMEM((rows, LANES), F32), pltpu.VMEM((rows, LANES), F32),
                            pltpu.VMEM((rows, LANES), F32)]),
        compiler_params=_params(("arbitrary",)),
        name="fox_decode",
    )(page_table, q, kvnew, lfnew, cache_kv, cache_lf_t)


def _pad_new_rows(a, n_seq, t_new):
    a = a.reshape(n_seq, t_new, a.shape[-1])
    return jnp.pad(a, ((0, 0), (0, NEW_ROWS_PAD - t_new), (0, 0)))


def _heads_major(a, n_seq, t_new, heads):
    a = a.reshape(n_seq, t_new, heads, LANES).swapaxes(1, 2)
    a = jnp.pad(a, ((0, 0), (0, 0), (0, NEW_ROWS_PAD - t_new), (0, 0)))
    return a.reshape(n_seq, heads * NEW_ROWS_PAD, LANES)


def _tokens_major(o, n_seq, t_new, heads):
    return o.reshape(n_seq, heads, NEW_ROWS_PAD, LANES)[:, :, :t_new].swapaxes(1, 2)


def _pick_tile(n, pref):
    t = min(pref, n)
    while n % t:
        t //= 2
    return t


def _trunk(x, mods, tok, tabs, n_seq, seq_len, weights, depth, paged):
    alpha = (2 * depth) ** 0.25
    rows_mla, rows_diff, rows_kv, rows_lf = [], [], [], []
    e_mat = _fox_lane_matrices()
    tq = _pick_tile(seq_len, ATTN_TQ)
    wide = max(tq, _pick_tile(seq_len, ATTN_WIDE))
    for l in range(depth):
        mod = mods[l]
        i = l // 2
        if l % 2 == 0:
            w = weights['even'][i]
            lam_init = 0.8 - 0.6 * math.exp(-0.3 * l)
            qmla, kmla, mrows, dq, drows, dkv = _even_in(tok, x, mod, tabs, w, weights['mla_g_q'][i], weights['mla_g_kv'][i])
            lam_w = [weights[k][i].reshape(1, DIFF_HD) for k in ('diff_lam_q1', 'diff_lam_k1', 'diff_lam_q2', 'diff_lam_k2')]
            g_sub = weights['diff_g_sub'][i].reshape(1, 2 * DIFF_HD)
            if paged is None:
                lat = _mla_prompt(qmla, kmla, n_seq, seq_len, tq, wide)
                odif = _diff_prompt(dq, dkv, lam_w, g_sub, n_seq, seq_len, tq, wide, lam_init)
            else:
                page_table, cache_mla_t, cache_diff = paged[0], paged[1], paged[2]
                lat = _mla_decode(page_table, qmla.reshape(n_seq, seq_len * MLA_HEADS, MLA_KW),
                                  _pad_new_rows(kmla, n_seq, seq_len), cache_mla_t, i)
                lat = lat.reshape(tok.n, MLA_HEADS * KV_LORA)
                odif = _diff_decode(page_table, _heads_major(dq, n_seq, seq_len, DIFF_HEADS),
                                    _pad_new_rows(dkv, n_seq, seq_len), lam_w, g_sub, cache_diff, i, lam_init)
                odif = _tokens_major(odif, n_seq, seq_len, DIFF_HEADS).reshape(tok.n, DIFF_HEADS * LANES)
            x = _even_out(tok, x, mod, lat, odif, w, weights['ln_g'][l, 0], weights['ln_b'][l, 0], alpha)
            rows_mla.append(mrows)
            rows_diff.append(drows)
        else:
            w = weights['odd'][i]
            q, kvrows, kvb, logf = _odd_in(tok, x, mod, w, weights['fox_b_f'][i])
            if paged is None:
                augk, prow = _fox_prefix(logf, n_seq, seq_len, e_mat)
                o_fox = _fox_prompt(q, kvb, augk, prow, n_seq, seq_len, tq, wide)
            else:
                page_table, cache_kv, cache_lf_t = paged[0], paged[3], paged[4]
                o = _fox_decode(page_table, _heads_major(q, n_seq, seq_len, FOX_HEADS),
                                _pad_new_rows(kvb, n_seq, seq_len), _pad_new_rows(logf, n_seq, seq_len),
                                cache_kv, cache_lf_t, i)
                o_fox = _tokens_major(o, n_seq, seq_len, FOX_HEADS)[..., FOX_HD:].reshape(tok.n, FOX_HEADS * FOX_HD)
            x = _odd_out(tok, x, mod, o_fox, w['w_out'], weights['ln_g'][l, 0], weights['ln_b'][l, 0], alpha)
            rows_kv.append(kvrows)
            rows_lf.append(logf)
        x = _mlp(tok, x, mod, weights['mlp_w_up'][l], weights['mlp_w_down'][l],
                 weights['ln_g'][l, 1], weights['ln_b'][l, 1], alpha)
    return x, rows_mla, rows_diff, rows_kv, rows_lf


def kernel(x_prompt, x_sample, cache_mla, cache_diff, cache_fox_kv, cache_fox_logf, page_table, c_prompt, c_sample, even_w_in, mla_g_q, mla_w_uq, mla_g_kv, mla_w_uk, mla_w_uv, diff_lam_q1, diff_lam_k1, diff_lam_q2, diff_lam_k2, diff_g_sub, even_w_out, fox_w_in, fox_b_f, fox_w_out, ada_w, ada_b, ln_g, ln_b, mlp_w_up, mlp_w_down):
    depth = ada_w.shape[0]
    bp, sp, d = x_prompt.shape
    bs, ts, _ = x_sample.shape
    n_pages = page_table.shape[1]
    past_len = n_pages * PAGE_SIZE
    assert d == D_MODEL and ts <= NEW_ROWS_PAD and cache_mla.shape[2] == PAGE_SIZE

    weights = dict(
        even=[_prep_even_weights(even_w_in[i], mla_w_uq[i], mla_w_uk[i], mla_w_uv[i], even_w_out[i])
              for i in range(even_w_in.shape[0])],
        odd=[_prep_odd_weights(fox_w_in[i], fox_w_out[i]) for i in range(fox_w_in.shape[0])],
        mla_g_q=mla_g_q, mla_g_kv=mla_g_kv, diff_lam_q1=diff_lam_q1, diff_lam_k1=diff_lam_k1,
        diff_lam_q2=diff_lam_q2, diff_lam_k2=diff_lam_k2, diff_g_sub=diff_g_sub, fox_b_f=fox_b_f,
        ln_g=ln_g, ln_b=ln_b, mlp_w_up=mlp_w_up.astype(BF16), mlp_w_down=mlp_w_down.astype(BF16))

    mod = _adaln(jnp.concatenate([c_prompt, c_sample], 0), ada_w, ada_b)
    mods_p = [mod[l, :bp].reshape(bp, 1, 6 * d) for l in range(depth)]
    mods_s = [jnp.repeat(mod[l, bp:], ts, axis=0).reshape(1, bs * ts, 6 * d) for l in range(depth)]

    tm_p = _pick_tile(sp, TOKEN_TILE)
    tok_p = _Tokens(bp * sp, tm_p, sp // tm_p, per_token=False)
    pos_p = jnp.arange(sp)
    tabs_p = _rope_tables(pos_p, DIFF_HD // 2) + _rope_tables(pos_p, MLA_ROPE // 2)
    y_p, mla_p, diff_p, kv_p, lf_p = _trunk(x_prompt.reshape(bp * sp, d), mods_p, tok_p, tabs_p, bp, sp,
                                            weights, depth, None)

    n_s = bs * ts
    tm_s = _pick_tile(n_s, 512)
    tok_s = _Tokens(n_s, tm_s, 1, per_token=True)
    pos_s = jnp.tile(past_len + jnp.arange(ts), bs)
    tabs_s = _rope_tables(pos_s, DIFF_HD // 2) + _rope_tables(pos_s, MLA_ROPE // 2)
    paged = (page_table, jnp.swapaxes(cache_mla, 2, 3), cache_diff, cache_fox_kv, jnp.swapaxes(cache_fox_logf, 2, 3))
    y_s, mla_s, diff_s, kv_s, lf_s = _trunk(x_sample.reshape(n_s, d), mods_s, tok_s, tabs_s, bs, ts,
                                            weights, depth, paged)

    def stack(rows, lead, tail):
        return jnp.stack([r.reshape(lead + tail) for r in rows])

    return (y_p.reshape(bp, sp, d), y_s.reshape(bs, ts, d),
            stack(mla_p, (bp, sp), (MLA_KUSED,)), stack(mla_s, (bs, ts), (MLA_KUSED,)),
            stack(diff_p, (bp, sp), (DIFF_KV_HEADS, 4 * DIFF_HD)), stack(diff_s, (bs, ts), (DIFF_KV_HEADS, 4 * DIFF_HD)),
            stack(kv_p, (bp, sp), (FOX_KV_HEADS, 2 * FOX_HD)), stack(kv_s, (bs, ts), (FOX_KV_HEADS, 2 * FOX_HD)),
            stack(lf_p, (bp, sp), (FOX_HEADS,)), stack(lf_s, (bs, ts), (FOX_HEADS,)))
```

```python
import functools
import math

import numpy as np
import jax
import jax.numpy as jnp
from jax import lax
from jax.experimental import pallas as pl
from jax.experimental.pallas import tpu as pltpu

F32, BF16, I32 = jnp.float32, jnp.bfloat16, jnp.int32
SDS = jax.ShapeDtypeStruct

D_MODEL = 1024
PAGE_SIZE = 128
MLA_HEADS, MLA_NOPE, MLA_ROPE, MLA_V = 8, 64, 32, 64
Q_LORA, KV_LORA = 256, 128
DIFF_HEADS, DIFF_KV_HEADS, DIFF_HD = 4, 2, 64
FOX_HEADS, FOX_KV_HEADS, FOX_HD = 16, 4, 64
D_FF = 4 * D_MODEL
ROPE_THETA = 10000.0
RMS_EPS = 1e-6
LN_EPS = 1e-5
MLA_SCALE = (MLA_NOPE + MLA_ROPE) ** -0.5
DIFF_SCALE = DIFF_HD ** -0.5
FOX_SCALE = FOX_HD ** -0.5
EVEN_SPLITS = (Q_LORA, KV_LORA, MLA_ROPE, DIFF_HEADS * 2 * DIFF_HD, DIFF_KV_HEADS * 2 * DIFF_HD,
               DIFF_KV_HEADS * 2 * DIFF_HD)
ODD_SPLITS = (FOX_HEADS * FOX_HD, FOX_KV_HEADS * FOX_HD, FOX_KV_HEADS * FOX_HD, FOX_HEADS)

LANES = 128
SUBLANES = 8
VMEM_LIMIT_BYTES = 56 * 1024 * 1024

LOG2E = 1.4426950408889634
NEG_BIG = -1e30
MLA_KW = 256
MLA_KUSED = KV_LORA + MLA_ROPE
FOX_HI, FOX_MID, FOX_LO, FOX_ONE = 64, 80, 96, 112
NEW_ROWS_PAD = 8
ATTN_TQ = 512
MLA_DECODE_PAGES = 64
KV_DECODE_PAGES = 32
TOKEN_TILE = 1024
ATTN_WIDE = 1024


def _params(sem):
    return pltpu.CompilerParams(dimension_semantics=sem, vmem_limit_bytes=VMEM_LIMIT_BYTES)


def _const_spec(shape):
    nd = len(shape)
    return pl.BlockSpec(shape, lambda *_: (0,) * nd, pipeline_mode=pl.Buffered(1))


def _rms(x, g):
    return x * lax.rsqrt(jnp.mean(x * x, axis=-1, keepdims=True) + RMS_EPS) * g


def _layer_norm(x, g, b):
    mu = jnp.mean(x, axis=-1, keepdims=True)
    xc = x - mu
    var = jnp.mean(xc * xc, axis=-1, keepdims=True)
    return xc * lax.rsqrt(var + LN_EPS) * g + b


def _rope_chunk(x, cos, sin_signed, half):
    fwd = pltpu.roll(x, LANES - half, 1)
    bwd = pltpu.roll(x, half, 1)
    lane = lax.broadcasted_iota(I32, x.shape, 1)
    swapped = jnp.where((lane & half) == 0, fwd, bwd)
    return x * cos + swapped * sin_signed


def _split3(x):
    hi = x.astype(BF16)
    r1 = x - hi.astype(F32)
    mid = r1.astype(BF16)
    lo = (r1 - mid.astype(F32)).astype(BF16)
    return hi, mid, lo


def _dot(a, b):
    return jnp.dot(a, b, preferred_element_type=F32)


def _dot_nt(a, b):
    return lax.dot_general(a, b, (((1,), (1,)), ((), ())), preferred_element_type=F32)


def _lanes_to(x, n):
    if n % LANES == 0:
        return jnp.tile(x, (1, n // LANES)) if n > LANES else x
    return x[:, 0:n]


def _softmax_update(s, pv, m_ref, l_ref, acc_ref, row_term=None):
    m_prev = m_ref[...]
    m_cur = jnp.max(s, axis=1, keepdims=True)
    if row_term is not None:
        m_cur = m_cur + row_term
    m_new = jnp.maximum(m_prev, m_cur)
    alpha = jnp.exp2(m_prev - m_new)
    shift = m_new if row_term is None else m_new - row_term
    p = jnp.exp2(s - _lanes_to(shift, s.shape[1]))
    l_ref[...] = alpha * l_ref[...] + jnp.sum(p, axis=1, keepdims=True)
    acc_ref[...] = alpha * acc_ref[...] + pv(p.astype(BF16))
    m_ref[...] = m_new


def _init_softmax(m_ref, l_ref, acc_ref):
    m_ref[...] = jnp.full(m_ref.shape, NEG_BIG, F32)
    l_ref[...] = jnp.zeros(l_ref.shape, F32)
    acc_ref[...] = jnp.zeros(acc_ref.shape, F32)


def _adaln_kernel(c_ref, w_ref, b_ref, o_ref):
    c = c_ref[...]
    s = (c * jax.nn.sigmoid(c)).astype(BF16)
    o_ref[...] = _dot(s, w_ref[...].astype(BF16)) + b_ref[...]


def _adaln(c_all, ada_w, ada_b):
    n_layers, d, d6 = ada_w.shape
    bc = c_all.shape[0]
    tn = d6 // 4
    return pl.pallas_call(
        _adaln_kernel,
        out_shape=SDS((n_layers, bc, d6), F32),
        grid=(n_layers, d6 // tn),
        in_specs=[pl.BlockSpec((bc, d), lambda l, j: (0, 0)),
                  pl.BlockSpec((None, d, tn), lambda l, j: (l, 0, j)),
                  pl.BlockSpec((None, 1, tn), lambda l, j: (l, 0, j))],
        out_specs=pl.BlockSpec((None, bc, tn), lambda l, j: (l, 0, j)),
        compiler_params=_params(("arbitrary", "arbitrary")),
        name="adaln",
    )(c_all, ada_w, ada_b.reshape(n_layers, 1, d6))


class _Tokens:
    def __init__(self, n_tok, tile, tiles_per_seq, per_token):
        self.n = n_tok
        self.tm = tile
        self.tps = tiles_per_seq
        self.per_token = per_token
        self.grid = (n_tok // tile,)

    def rows(self, width):
        return pl.BlockSpec((self.tm, width), lambda i: (i, 0))

    def rows3(self, heads, width):
        return pl.BlockSpec((self.tm, heads, width), lambda i: (i, 0, 0))

    def mod(self, chunk):
        if self.per_token:
            return pl.BlockSpec((None, self.tm, D_MODEL), lambda i: (0, i, chunk))
        tps = self.tps
        return pl.BlockSpec((None, 1, D_MODEL), lambda i: (i // tps, 0, chunk))

    def pos(self):
        if self.per_token:
            return pl.BlockSpec((self.tm, LANES), lambda i: (i, 0))
        tps = self.tps
        return pl.BlockSpec((self.tm, LANES), lambda i: (i % tps, 0))


def _rope_tables(pos, half):
    inv = ROPE_THETA ** (-jnp.arange(half, dtype=F32) / half)
    ang = pos.astype(F32)[:, None] * inv[None, :]
    cos, sin = jnp.cos(ang), jnp.sin(ang)
    reps = LANES // (2 * half)
    return (jnp.tile(jnp.concatenate([cos, cos], -1), (1, reps)),
            jnp.tile(jnp.concatenate([-sin, sin], -1), (1, reps)))


_EV_CQ, _EV_CKV, _EV_DQ, _EV_DK, _EV_DV, _EV_KPE, _EV_END = 0, 256, 384, 896, 1152, 1408, 1536
_UQ_ROPE = MLA_HEADS * LANES


def _even_in_kernel(x_ref, sh_ref, sc_ref, c64_ref, s64_ref, c32_ref, s32_ref, win_ref, gq_ref, gkv_ref,
                    wuq_ref, wuk_ref, qmla_ref, kmla_ref, mrows_ref, dq_ref, drows_ref, dkv_ref):
    h = (x_ref[...] * (1.0 + sc_ref[...]) + sh_ref[...]).astype(BF16)
    y = _dot(h, win_ref[...])
    c64, s64, c32, s32 = c64_ref[...], s64_ref[...], c32_ref[...], s32_ref[...]
    lane = lax.broadcasted_iota(I32, c64.shape, 1)
    q_scale = MLA_SCALE * LOG2E

    cq = _rms(y[:, _EV_CQ:_EV_CKV], gq_ref[...]).astype(BF16)
    q = _dot(cq, wuq_ref[...])
    qpe = [_rope_chunk(q[:, _UQ_ROPE + LANES * j:_UQ_ROPE + LANES * (j + 1)], c32, s32, MLA_ROPE // 2) * q_scale
           for j in range(MLA_HEADS * MLA_ROPE // LANES)]
    heads_per_chunk = LANES // MLA_ROPE
    for hd in range(MLA_HEADS):
        q_lat = _dot(q[:, LANES * hd:LANES * (hd + 1)].astype(BF16), wuk_ref[hd]) * q_scale
        qmla_ref[:, MLA_KW * hd:MLA_KW * hd + LANES] = q_lat.astype(BF16)
        chunk = qpe[hd // heads_per_chunk]
        shift = (hd % heads_per_chunk) * MLA_ROPE
        if shift:
            chunk = pltpu.roll(chunk, LANES - shift, 1)
        qmla_ref[:, MLA_KW * hd + LANES:MLA_KW * (hd + 1)] = jnp.where(lane < MLA_ROPE, chunk, 0.0).astype(BF16)

    ckv = _rms(y[:, _EV_CKV:_EV_DQ], gkv_ref[...])
    kpe = _rope_chunk(y[:, _EV_KPE:_EV_END], c32, s32, MLA_ROPE // 2)
    mrows_ref[:, 0:KV_LORA] = ckv
    mrows_ref[:, KV_LORA:MLA_KUSED] = kpe[:, 0:MLA_ROPE]
    kmla_ref[:, 0:LANES] = ckv.astype(BF16)
    kmla_ref[:, LANES:MLA_KW] = kpe.astype(BF16)

    d_scale = DIFF_SCALE * LOG2E
    for j in range(4):
        dq = _rope_chunk(y[:, _EV_DQ + LANES * j:_EV_DQ + LANES * (j + 1)], c64, s64, DIFF_HD // 2)
        dq_ref[:, LANES * j:LANES * (j + 1)] = (dq * d_scale).astype(BF16)
    for g in range(DIFF_KV_HEADS):
        dk = _rope_chunk(y[:, _EV_DK + LANES * g:_EV_DK + LANES * (g + 1)], c64, s64, DIFF_HD // 2)
        dv = y[:, _EV_DV + LANES * g:_EV_DV + LANES * (g + 1)]
        base = 2 * LANES * g
        drows_ref[:, g, 0:LANES] = dk
        drows_ref[:, g, LANES:2 * LANES] = dv
        dkv_ref[:, base:base + LANES] = dk.astype(BF16)
        dkv_ref[:, base + LANES:base + 2 * LANES] = dv.astype(BF16)


def _prep_even_weights(w_in, w_uq, w_uk, w_uv, w_out):
    o = np.cumsum((0,) + EVEN_SPLITS)
    c_q, c_kv, k_pe, dq, dk, dv = (w_in[:, o[j]:o[j + 1]] for j in range(6))
    d = w_in.shape[0]
    w_in_r = jnp.concatenate([c_q, c_kv, dq, dk, dv, k_pe, jnp.zeros((d, LANES - MLA_ROPE), w_in.dtype)], 1)
    nope = jnp.concatenate([w_uq[:, :, :MLA_NOPE], jnp.zeros((Q_LORA, MLA_HEADS, LANES - MLA_NOPE), w_uq.dtype)], -1)
    w_uq_r = jnp.concatenate([nope.reshape(Q_LORA, MLA_HEADS * LANES),
                              w_uq[:, :, MLA_NOPE:].reshape(Q_LORA, MLA_HEADS * MLA_ROPE)], 1)
    wk = jnp.transpose(w_uk, (1, 2, 0))
    wk = jnp.concatenate([wk, jnp.zeros((MLA_HEADS, LANES - MLA_NOPE, KV_LORA), wk.dtype)], 1)
    eye = jnp.eye(MLA_HEADS, dtype=w_uv.dtype)
    wv = jnp.einsum('chd,hg->hcgd', w_uv, eye).reshape(MLA_HEADS * KV_LORA, MLA_HEADS * MLA_V)
    n_mla = MLA_HEADS * MLA_V
    return dict(w_in=w_in_r.astype(BF16), w_uq=w_uq_r.astype(BF16), w_uk=wk.astype(BF16), w_uv=wv.astype(BF16),
                w_out_mla=w_out[:n_mla].astype(BF16), w_out_dif=w_out[n_mla:].astype(BF16))


def _even_in(tok, x, mod, tabs, w, g_q, g_kv):
    c64, s64, c32, s32 = tabs
    n = tok.n
    outs = (SDS((n, MLA_HEADS * MLA_KW), BF16), SDS((n, MLA_KW), BF16), SDS((n, MLA_KUSED), F32),
            SDS((n, 4 * LANES), BF16), SDS((n, DIFF_KV_HEADS, 4 * DIFF_HD), F32), SDS((n, 4 * LANES), BF16))
    return pl.pallas_call(
        _even_in_kernel,
        out_shape=outs,
        grid=tok.grid,
        in_specs=[tok.rows(D_MODEL), tok.mod(0), tok.mod(1), tok.pos(), tok.pos(), tok.pos(), tok.pos(),
                  _const_spec(w['w_in'].shape), _const_spec((1, Q_LORA)), _const_spec((1, KV_LORA)),
                  _const_spec(w['w_uq'].shape), _const_spec(w['w_uk'].shape)],
        out_specs=[tok.rows(MLA_HEADS * MLA_KW), tok.rows(MLA_KW), tok.rows(MLA_KUSED),
                   tok.rows(4 * LANES), tok.rows3(DIFF_KV_HEADS, 4 * DIFF_HD), tok.rows(4 * LANES)],
        compiler_params=_params(("arbitrary",)),
        name="even_in",
    )(x, mod, mod, c64, s64, c32, s32, w['w_in'], g_q.reshape(1, Q_LORA), g_kv.reshape(1, KV_LORA),
      w['w_uq'], w['w_uk'])


MLP_FF_CHUNK = 1024


def _residual_norm(x, gate, y, lng_ref, lnb_ref, alpha):
    return _layer_norm(alpha * x + (1.0 + gate) * y, lng_ref[...], lnb_ref[...])


def _mlp_block(x, sh_ref, sc_ref, g_ref, wup_ref, wdn_ref, lng_ref, lnb_ref, alpha):
    h = (x * (1.0 + sc_ref[...]) + sh_ref[...]).astype(BF16)
    y = jnp.zeros(x.shape, F32)
    for f in range(D_FF // MLP_FF_CHUNK):
        cols = slice(f * MLP_FF_CHUNK, (f + 1) * MLP_FF_CHUNK)
        u = jnp.maximum(_dot(h, wup_ref[:, cols]), 0.0)
        y = y + _dot((u * u).astype(BF16), wdn_ref[cols, :])
    return _residual_norm(x, g_ref[...], y, lng_ref, lnb_ref, alpha)


def _even_out_mlp_kernel(x_ref, g1_ref, lat_ref, odif_ref, wuv_ref, wo1_ref, wo2_ref, lng1_ref, lnb1_ref,
                         sh2_ref, sc2_ref, g2_ref, wup_ref, wdn_ref, lng2_ref, lnb2_ref, o_ref, *, alpha):
    o_mla = _dot(lat_ref[...], wuv_ref[...]).astype(BF16)
    y = _dot(o_mla, wo1_ref[...]) + _dot(odif_ref[...], wo2_ref[...])
    x1 = _residual_norm(x_ref[...], g1_ref[...], y, lng1_ref, lnb1_ref, alpha)
    o_ref[...] = _mlp_block(x1, sh2_ref, sc2_ref, g2_ref, wup_ref, wdn_ref, lng2_ref, lnb2_ref, alpha)


def _odd_out_mlp_kernel(x_ref, g1_ref, o_in_ref, wo_ref, lng1_ref, lnb1_ref,
                        sh2_ref, sc2_ref, g2_ref, wup_ref, wdn_ref, lng2_ref, lnb2_ref, o_ref, *, alpha):
    x1 = _residual_norm(x_ref[...], g1_ref[...], _dot(o_in_ref[...], wo_ref[...]), lng1_ref, lnb1_ref, alpha)
    o_ref[...] = _mlp_block(x1, sh2_ref, sc2_ref, g2_ref, wup_ref, wdn_ref, lng2_ref, lnb2_ref, alpha)


def _mixer_out_mlp(tok, x, mod, mixed, mix_weights, w_up, w_down, ln_g, ln_b, alpha, even):
    vec = _const_spec((1, D_MODEL))
    body = _even_out_mlp_kernel if even else _odd_out_mlp_kernel
    return pl.pallas_call(
        functools.partial(body, alpha=alpha),
        out_shape=SDS((tok.n, D_MODEL), F32),
        grid=tok.grid,
        in_specs=[tok.rows(D_MODEL), tok.mod(2)] + [tok.rows(a.shape[1]) for a in mixed]
                 + [_const_spec(w.shape) for w in mix_weights] + [vec, vec, tok.mod(3), tok.mod(4), tok.mod(5),
                                                                  _const_spec(w_up.shape), _const_spec(w_down.shape), vec, vec],
        out_specs=tok.rows(D_MODEL),
        compiler_params=_params(("arbitrary",)),
        name="even_out_mlp" if even else "odd_out_mlp",
    )(x, mod, *mixed, *mix_weights, ln_g[0].reshape(1, -1), ln_b[0].reshape(1, -1), mod, mod, mod, w_up, w_down,
      ln_g[1].reshape(1, -1), ln_b[1].reshape(1, -1))


_OD_KV = FOX_HEADS * FOX_HD
_OD_F = _OD_KV + 2 * FOX_KV_HEADS * FOX_HD
_OD_END = _OD_F + LANES


def _odd_in_kernel(x_ref, sh_ref, sc_ref, win_ref, bf_ref, q_ref, kvrows_ref, kvb_ref, logf_ref):
    h = (x_ref[...] * (1.0 + sc_ref[...]) + sh_ref[...]).astype(BF16)
    y = _dot(h, win_ref[...])
    lane = lax.broadcasted_iota(I32, (y.shape[0], LANES), 1)
    scale = FOX_SCALE * LOG2E
    for hd in range(FOX_HEADS):
        chunk = y[:, LANES * (hd // 2):LANES * (hd // 2 + 1)]
        if hd % 2:
            chunk = pltpu.roll(chunk, FOX_HD, 1)
        q_ref[:, LANES * hd:LANES * (hd + 1)] = jnp.where(lane < FOX_HD, chunk * scale, 0.0).astype(BF16)
    kv = y[:, _OD_KV:_OD_F]
    for g in range(FOX_KV_HEADS):
        kvrows_ref[:, g, :] = kv[:, LANES * g:LANES * (g + 1)]
    kvb_ref[...] = kv.astype(BF16)
    z = y[:, _OD_F:_OD_F + FOX_HEADS] + bf_ref[...]
    logf_ref[...] = jnp.minimum(z, 0.0) - jnp.log1p(jnp.exp(-jnp.abs(z)))


def _prep_odd_weights(w_in, w_out):
    o = np.cumsum((0,) + ODD_SPLITS)
    q, k, v, f = (w_in[:, o[j]:o[j + 1]] for j in range(4))
    d = w_in.shape[0]
    k = k.reshape(d, FOX_KV_HEADS, FOX_HD)
    v = v.reshape(d, FOX_KV_HEADS, FOX_HD)
    kv = jnp.concatenate([k, v], -1).reshape(d, 2 * FOX_KV_HEADS * FOX_HD)
    w_in_r = jnp.concatenate([q, kv, f, jnp.zeros((d, LANES - FOX_HEADS), w_in.dtype)], 1)
    return dict(w_in=w_in_r.astype(BF16), w_out=w_out.astype(BF16))


def _odd_in(tok, x, mod, w, b_f):
    n = tok.n
    outs = (SDS((n, FOX_HEADS * LANES), BF16), SDS((n, FOX_KV_HEADS, 2 * FOX_HD), F32), SDS((n, 4 * LANES), BF16),
            SDS((n, FOX_HEADS), F32))
    return pl.pallas_call(
        _odd_in_kernel,
        out_shape=outs,
        grid=tok.grid,
        in_specs=[tok.rows(D_MODEL), tok.mod(0), tok.mod(1), _const_spec(w['w_in'].shape), _const_spec((1, FOX_HEADS))],
        out_specs=[tok.rows(FOX_HEADS * LANES), tok.rows3(FOX_KV_HEADS, 2 * FOX_HD), tok.rows(4 * LANES),
                   tok.rows(FOX_HEADS)],
        compiler_params=_params(("arbitrary",)),
        name="odd_in",
    )(x, mod, mod, w['w_in'], b_f.reshape(1, FOX_HEADS))


PREFIX_BLOCK = 256


def _fox_lane_matrices():
    e = np.zeros((3, FOX_HEADS, LANES), np.float32)
    for part, base in enumerate((FOX_HI, FOX_MID, FOX_LO)):
        for hd in range(FOX_HEADS):
            e[part, hd, base + hd] = 1.0
    return jnp.asarray(e, BF16)


def _ones_lanes(rows):
    lane = lax.broadcasted_iota(I32, (rows, LANES), 1)
    return jnp.where((lane >= FOX_ONE) & (lane < FOX_ONE + 3), 1.0, 0.0)


def _place_key_terms(v, e_ref):
    hi, mid, lo = _split3(v)
    a = _dot(hi, e_ref[0]) + _dot(mid, e_ref[1]) + _dot(lo, e_ref[2]) + _ones_lanes(v.shape[0])
    return a.astype(BF16)


def _query_extras(col, head, rows):
    lane = lax.broadcasted_iota(I32, (rows, LANES), 1)
    hi = col.astype(BF16).astype(F32)
    r1 = col - hi
    mid = r1.astype(BF16).astype(F32)
    lo = r1 - mid
    sel = jnp.where((lane == FOX_HI + head) | (lane == FOX_MID + head) | (lane == FOX_LO + head), 1.0, 0.0)
    return jnp.where(lane == FOX_ONE, hi, jnp.where(lane == FOX_ONE + 1, mid, jnp.where(lane == FOX_ONE + 2, lo, sel)))


def _fox_prefix_kernel(lf_ref, e_ref, augk_ref, prow_ref):
    n = lf_ref.shape[0]
    r = lax.broadcasted_iota(I32, (PREFIX_BLOCK, PREFIX_BLOCK), 0)
    c = lax.broadcasted_iota(I32, (PREFIX_BLOCK, PREFIX_BLOCK), 1)
    tri = jnp.where(r >= c, 1.0, 0.0).astype(BF16)
    carry = jnp.zeros((1, FOX_HEADS), F32)
    for blk in range(n // PREFIX_BLOCK):
        rows = pl.ds(blk * PREFIX_BLOCK, PREFIX_BLOCK)
        hi, mid, lo = _split3(lf_ref[rows, :])
        p = _dot(tri, hi) + _dot(tri, mid) + _dot(tri, lo) + carry
        carry = p[PREFIX_BLOCK - 1:PREFIX_BLOCK, :]
        p2 = p * LOG2E
        prow_ref[rows, :] = p2
        augk_ref[rows, :] = _place_key_terms(-p2, e_ref)


def _fox_prefix(logf, n_seq, seq_len, e_mat):
    n = logf.shape[0]
    return pl.pallas_call(
        _fox_prefix_kernel,
        out_shape=(SDS((n, LANES), BF16), SDS((n, FOX_HEADS), F32)),
        grid=(n_seq,),
        in_specs=[pl.BlockSpec((seq_len, FOX_HEADS), lambda b: (b, 0)), _const_spec(e_mat.shape)],
        out_specs=[pl.BlockSpec((seq_len, LANES), lambda b: (b, 0)), pl.BlockSpec((seq_len, FOX_HEADS), lambda b: (b, 0))],
        compiler_params=_params(("arbitrary",)),
        name="fox_prefix",
    )(logf, e_mat)


def _causal_sweep(i, tq, wide, step):
    ratio = wide // tq
    n_wide = lax.div(i, ratio)
    n_rem = i - n_wide * ratio

    def wide_body(j, carry):
        step(pl.multiple_of(j * wide, wide), wide, False)
        return carry

    def rem_body(j, carry):
        step(pl.multiple_of((n_wide * ratio + j) * tq, tq), tq, False)
        return carry

    lax.fori_loop(0, n_wide, wide_body, 0)
    lax.fori_loop(0, n_rem, rem_body, 0)
    step(pl.multiple_of(i * tq, tq), tq, True)


def _diag_mask(rows, tq, token_of_row):
    r = lax.broadcasted_iota(I32, (rows, tq), 0)
    c = lax.broadcasted_iota(I32, (rows, tq), 1)
    return jnp.where(c <= token_of_row(r), 0.0, NEG_BIG)


def _mla_prompt_kernel(q_ref, k_ref, o_ref, qs_ref, m_ref, l_ref, acc_ref, *, tq, wide):
    i = pl.program_id(1)
    _init_softmax(m_ref, l_ref, acc_ref)
    for hd in range(MLA_HEADS):
        qs_ref[hd * tq:(hd + 1) * tq, :] = q_ref[:, MLA_KW * hd:MLA_KW * (hd + 1)]
    q = qs_ref[...]
    mask = jnp.tile(_diag_mask(tq, tq, lambda r: r), (MLA_HEADS, 1))

    def step(off, width, masked):
        kb = k_ref[pl.ds(off, width), :]
        s = _dot_nt(q, kb)
        if masked:
            s = s + mask
        _softmax_update(s, lambda p: _dot(p, kb[:, 0:KV_LORA]), m_ref, l_ref, acc_ref)

    _causal_sweep(i, tq, wide, step)
    o = (acc_ref[...] / l_ref[...]).astype(BF16)
    for hd in range(MLA_HEADS):
        o_ref[:, KV_LORA * hd:KV_LORA * (hd + 1)] = o[hd * tq:(hd + 1) * tq, :]


def _mla_prompt(q, k_rows, n_seq, seq_len, tq, wide):
    nq = seq_len // tq
    rq = tq * MLA_HEADS
    return pl.pallas_call(
        functools.partial(_mla_prompt_kernel, tq=tq, wide=wide),
        out_shape=SDS((q.shape[0], MLA_HEADS * KV_LORA), BF16),
        grid=(n_seq, nq),
        in_specs=[pl.BlockSpec((tq, MLA_HEADS * MLA_KW), lambda b, i: (b * nq + i, 0)),
                  pl.BlockSpec((seq_len, MLA_KW), lambda b, i: (b, 0))],
        out_specs=pl.BlockSpec((tq, MLA_HEADS * KV_LORA), lambda b, i: (b * nq + i, 0)),
        scratch_shapes=[pltpu.VMEM((rq, MLA_KW), BF16), pltpu.VMEM((rq, LANES), F32), pltpu.VMEM((rq, LANES), F32),
                        pltpu.VMEM((rq, KV_LORA), F32)],
        compiler_params=_params(("arbitrary", "arbitrary")),
        name="mla_prompt",
    )(q, k_rows)


def _diff_lambda(lq1_ref, lk1_ref, lq2_ref, lk2_ref, lam_init):
    a = jnp.exp(jnp.sum(lq1_ref[...] * lk1_ref[...], axis=1, keepdims=True))
    b = jnp.exp(jnp.sum(lq2_ref[...] * lk2_ref[...], axis=1, keepdims=True))
    return a - b + lam_init


def _diff_prompt_kernel(q_ref, kv_ref, lq1_ref, lk1_ref, lq2_ref, lk2_ref, gsub_ref, o_ref,
                        qs_ref, m_ref, l_ref, acc_ref, *, tq, wide, lam_init):
    i = pl.program_id(2)
    _init_softmax(m_ref, l_ref, acc_ref)
    lane = lax.broadcasted_iota(I32, (tq, LANES), 1)
    zero = jnp.zeros((tq, LANES), BF16)
    for hd in range(2):
        chunk = q_ref[:, LANES * hd:LANES * (hd + 1)]
        qs_ref[hd * tq:(hd + 1) * tq, :] = jnp.where(lane < DIFF_HD, chunk, zero)
        qs_ref[(2 + hd) * tq:(3 + hd) * tq, :] = jnp.where(lane >= DIFF_HD, chunk, zero)
    q = qs_ref[...]
    mask = jnp.tile(_diag_mask(tq, tq, lambda r: r), (4, 1))

    def step(off, width, masked):
        kb = kv_ref[pl.ds(off, width), 0:LANES]
        vb = kv_ref[pl.ds(off, width), LANES:2 * LANES]
        s = _dot_nt(q, kb)
        if masked:
            s = s + mask
        _softmax_update(s, lambda p: _dot(p, vb), m_ref, l_ref, acc_ref)

    _causal_sweep(i, tq, wide, step)
    o = acc_ref[...] / l_ref[...]
    lam = _diff_lambda(lq1_ref, lk1_ref, lq2_ref, lk2_ref, lam_init)
    for hd in range(2):
        d = o[hd * tq:(hd + 1) * tq, :] - lam * o[(2 + hd) * tq:(3 + hd) * tq, :]
        o_ref[:, LANES * hd:LANES * (hd + 1)] = (_rms(d, gsub_ref[...]) * (1.0 - lam_init)).astype(BF16)


def _lam_specs():
    return [_const_spec((1, DIFF_HD))] * 4 + [_const_spec((1, 2 * DIFF_HD))]


def _diff_prompt(dq, dkv, lam_w, g_sub, n_seq, seq_len, tq, wide, lam_init):
    nq = seq_len // tq
    return pl.pallas_call(
        functools.partial(_diff_prompt_kernel, tq=tq, wide=wide, lam_init=lam_init),
        out_shape=SDS((dq.shape[0], 4 * LANES), BF16),
        grid=(n_seq, DIFF_KV_HEADS, nq),
        in_specs=[pl.BlockSpec((tq, 2 * LANES), lambda b, g, i: (b * nq + i, g)),
                  pl.BlockSpec((seq_len, 2 * LANES), lambda b, g, i: (b, g))] + _lam_specs(),
        out_specs=pl.BlockSpec((tq, 2 * LANES), lambda b, g, i: (b * nq + i, g)),
        scratch_shapes=[pltpu.VMEM((4 * tq, LANES), BF16), pltpu.VMEM((4 * tq, LANES), F32),
                        pltpu.VMEM((4 * tq, LANES), F32), pltpu.VMEM((4 * tq, LANES), F32)],
        compiler_params=_params(("arbitrary", "arbitrary", "arbitrary")),
        name="diff_prompt",
    )(dq, dkv, *lam_w, g_sub)


def _fox_prompt_kernel(q_ref, kv_ref, augk_ref, prow_ref, o_ref, kaug_ref, qs_ref, m_ref, l_ref, acc_ref, *, tq, wide):
    i = pl.program_id(1)
    heads_per_kv = FOX_HEADS // FOX_KV_HEADS

    @pl.when(i == 0)
    def _():
        lane_s = lax.broadcasted_iota(I32, augk_ref.shape, 1)
        a = augk_ref[...]
        for g in range(FOX_KV_HEADS):
            kaug_ref[g] = jnp.where(lane_s < FOX_HD, kv_ref[:, LANES * g:LANES * (g + 1)], a)

    prow = prow_ref[...]
    lane = lax.broadcasted_iota(I32, (tq, LANES), 1)
    mask = jnp.tile(_diag_mask(tq, tq, lambda r: r), (heads_per_kv, 1))
    for g in range(FOX_KV_HEADS):
        for r in range(heads_per_kv):
            hd = g * heads_per_kv + r
            extras = _query_extras(prow[:, hd:hd + 1], hd, tq)
            qh = q_ref[:, LANES * hd:LANES * (hd + 1)].astype(F32)
            qs_ref[r * tq:(r + 1) * tq, :] = (qh + extras).astype(BF16)
        q = qs_ref[...]
        _init_softmax(m_ref, l_ref, acc_ref)

        def step(off, width, masked, g=g, q=q):
            kb = kaug_ref[g, pl.ds(off, width), :]
            vb = kv_ref[pl.ds(off, width), LANES * g:LANES * (g + 1)]
            s = _dot_nt(q, kb)
            if masked:
                s = s + mask
            _softmax_update(s, lambda p: _dot(p, vb), m_ref, l_ref, acc_ref)

        _causal_sweep(i, tq, wide, step)
        o = acc_ref[...] / l_ref[...]
        for pair in range(heads_per_kv // 2):
            even = pltpu.roll(o[(2 * pair) * tq:(2 * pair + 1) * tq, :], FOX_HD, 1)
            odd = o[(2 * pair + 1) * tq:(2 * pair + 2) * tq, :]
            col = (g * (heads_per_kv // 2) + pair) * LANES
            o_ref[:, col:col + LANES] = jnp.where(lane < FOX_HD, even, odd).astype(BF16)


def _fox_prompt(q, kvb, augk, prow, n_seq, seq_len, tq, wide):
    nq = seq_len // tq
    rows = (FOX_HEADS // FOX_KV_HEADS) * tq
    return pl.pallas_call(
        functools.partial(_fox_prompt_kernel, tq=tq, wide=wide),
        out_shape=SDS((q.shape[0], FOX_HEADS * FOX_HD), BF16),
        grid=(n_seq, nq),
        in_specs=[pl.BlockSpec((tq, FOX_HEADS * LANES), lambda b, i: (b * nq + i, 0)),
                  pl.BlockSpec((seq_len, 4 * LANES), lambda b, i: (b, 0)),
                  pl.BlockSpec((seq_len, LANES), lambda b, i: (b, 0)),
                  pl.BlockSpec((tq, FOX_HEADS), lambda b, i: (b * nq + i, 0))],
        out_specs=pl.BlockSpec((tq, FOX_HEADS * FOX_HD), lambda b, i: (b * nq + i, 0)),
        scratch_shapes=[pltpu.VMEM((FOX_KV_HEADS, seq_len, LANES), BF16), pltpu.VMEM((rows, LANES), BF16),
                        pltpu.VMEM((rows, LANES), F32), pltpu.VMEM((rows, LANES), F32), pltpu.VMEM((rows, LANES), F32)],
        compiler_params=_params(("arbitrary", "arbitrary")),
        name="fox_prompt",
    )(q, kvb, augk, prow)


def _pages_per_step(n_pages, pref):
    return max(1, min(pref, n_pages // 2))


class _PageStream:
    def __init__(self, pt_ref, cp, nch, reverse):
        self.pt = pt_ref
        self.cp = cp
        self.nch = nch
        self.reverse = reverse
        self.c = pl.program_id(0)
        self.n = pl.num_programs(0)
        self.slot = lax.rem(self.c, 2)
        self.chunk = lax.rem(self.c, nch)

    def _first_page(self, step):
        seq = step // self.nch
        ch = lax.rem(step, self.nch)
        if self.reverse:
            ch = self.nch - 1 - ch
        return seq, ch * self.cp

    def copies(self, step, slot, make):
        seq, first = self._first_page(step)
        out = []
        for p in range(self.cp):
            out.extend(make(self.pt[seq, first + p], p, slot))
        return out

    def prefetch(self, make):
        @pl.when(self.c == 0)
        def _():
            for cp_ in self.copies(self.c, self.slot, make):
                cp_.start()

        @pl.when(self.c + 1 < self.n)
        def _():
            for cp_ in self.copies(self.c + 1, 1 - self.slot, make):
                cp_.start()

    def wait(self, make):
        for cp_ in self.copies(self.c, self.slot, make):
            cp_.wait()


def _mla_decode_kernel(pt_ref, q_ref, knew_ref, cache_ref, o_ref, buf, sem, m_ref, l_ref, acc_ref, *, layer, cp, nch):
    ps = _PageStream(pt_ref, cp, nch, reverse=False)

    def make(page, p, slot):
        return [pltpu.make_async_copy(cache_ref.at[layer, page], buf.at[slot, :, pl.ds(p * PAGE_SIZE, PAGE_SIZE)],
                                      sem.at[slot])]

    ps.prefetch(make)
    q = q_ref[:, 0:MLA_KUSED]
    rows = q.shape[0]

    @pl.when(ps.chunk == 0)
    def _():
        _init_softmax(m_ref, l_ref, acc_ref)
        kn = knew_ref[...]
        s = _dot_nt(q, kn[:, 0:MLA_KUSED])
        t = lax.shift_right_logical(lax.broadcasted_iota(I32, (rows, NEW_ROWS_PAD), 0), 3)
        u = lax.broadcasted_iota(I32, (rows, NEW_ROWS_PAD), 1)
        s = jnp.where(u <= t, s, NEG_BIG)
        _softmax_update(s, lambda p: _dot(p, kn[:, 0:KV_LORA]), m_ref, l_ref, acc_ref)

    ps.wait(make)
    kt = buf[ps.slot].astype(BF16)
    s = _dot(q, kt)
    _softmax_update(s, lambda p: _dot_nt(p, kt[0:KV_LORA, :]), m_ref, l_ref, acc_ref)

    @pl.when(ps.chunk == nch - 1)
    def _():
        o_ref[...] = (acc_ref[...] / l_ref[...]).astype(BF16)


def _mla_decode(page_table, q, knew, cache_t, layer):
    n_seq, n_pages = page_table.shape
    cp = _pages_per_step(n_pages, MLA_DECODE_PAGES)
    nch = n_pages // cp
    rows = q.shape[1]
    return pl.pallas_call(
        functools.partial(_mla_decode_kernel, layer=layer, cp=cp, nch=nch),
        out_shape=SDS((n_seq, rows, KV_LORA), BF16),
        grid_spec=pltpu.PrefetchScalarGridSpec(
            num_scalar_prefetch=1,
            grid=(n_seq * nch,),
            in_specs=[pl.BlockSpec((None, rows, MLA_KW), lambda c, pt: (c // nch, 0, 0)),
                      pl.BlockSpec((None, NEW_ROWS_PAD, MLA_KW), lambda c, pt: (c // nch, 0, 0)),
                      pl.BlockSpec(memory_space=pl.ANY)],
            out_specs=pl.BlockSpec((None, rows, KV_LORA), lambda c, pt: (c // nch, 0, 0)),
            scratch_shapes=[pltpu.VMEM((2, MLA_KUSED, cp * PAGE_SIZE), F32), pltpu.SemaphoreType.DMA((2,)),
                            pltpu.VMEM((rows, LANES), F32), pltpu.VMEM((rows, LANES), F32), pltpu.VMEM((rows, KV_LORA), F32)]),
        compiler_params=_params(("arbitrary",)),
        name="mla_decode",
    )(page_table, q, knew, cache_t)


def _diff_decode_kernel(pt_ref, q_ref, knew_ref, lq1_ref, lk1_ref, lq2_ref, lk2_ref, gsub_ref, cache_ref, o_ref,
                        buf, sem, qs_ref, m_ref, l_ref, acc_ref, *, layer, cp, nch, lam_init):
    ps = _PageStream(pt_ref, cp, nch, reverse=False)
    per_kv = DIFF_HEADS // DIFF_KV_HEADS
    gq = per_kv * NEW_ROWS_PAD
    rows = 2 * DIFF_HEADS * NEW_ROWS_PAD

    def make(page, p, slot):
        return [pltpu.make_async_copy(cache_ref.at[layer, page, :, g, :],
                                      buf.at[slot, g, pl.ds(p * PAGE_SIZE, PAGE_SIZE), :], sem.at[slot])
                for g in range(DIFF_KV_HEADS)]

    ps.prefetch(make)

    def scores(key_of_group):
        q = qs_ref[...].astype(BF16)
        return jnp.concatenate([_dot_nt(q[2 * gq * g:2 * gq * (g + 1), :], key_of_group(g))
                                for g in range(DIFF_KV_HEADS)], axis=0)

    def values(value_of_group):
        return lambda p: jnp.concatenate([_dot(p[2 * gq * g:2 * gq * (g + 1), :], value_of_group(g))
                                          for g in range(DIFF_KV_HEADS)], axis=0)

    @pl.when(ps.chunk == 0)
    def _():
        lane = lax.broadcasted_iota(I32, (gq, LANES), 1)
        for g in range(DIFF_KV_HEADS):
            qg = q_ref[gq * g:gq * (g + 1), :].astype(F32)
            qs_ref[2 * gq * g:2 * gq * g + gq, :] = jnp.where(lane < DIFF_HD, qg, 0.0)
            qs_ref[2 * gq * g + gq:2 * gq * (g + 1), :] = jnp.where(lane >= DIFF_HD, qg, 0.0)
        _init_softmax(m_ref, l_ref, acc_ref)
        t = lax.broadcasted_iota(I32, (rows, NEW_ROWS_PAD), 0) & (NEW_ROWS_PAD - 1)
        u = lax.broadcasted_iota(I32, (rows, NEW_ROWS_PAD), 1)
        s = scores(lambda g: knew_ref[:, 2 * LANES * g:2 * LANES * g + LANES])
        s = jnp.where(u <= t, s, NEG_BIG)
        _softmax_update(s, values(lambda g: knew_ref[:, 2 * LANES * g + LANES:2 * LANES * (g + 1)]), m_ref, l_ref, acc_ref)

    ps.wait(make)
    kv = [buf[ps.slot, g] for g in range(DIFF_KV_HEADS)]
    s = scores(lambda g: kv[g][:, 0:LANES].astype(BF16))
    _softmax_update(s, values(lambda g: kv[g][:, LANES:2 * LANES].astype(BF16)), m_ref, l_ref, acc_ref)

    @pl.when(ps.chunk == nch - 1)
    def _():
        o = acc_ref[...] / l_ref[...]
        lam = _diff_lambda(lq1_ref, lk1_ref, lq2_ref, lk2_ref, lam_init)
        for g in range(DIFF_KV_HEADS):
            d = o[2 * gq * g:2 * gq * g + gq, :] - lam * o[2 * gq * g + gq:2 * gq * (g + 1), :]
            o_ref[gq * g:gq * (g + 1), :] = (_rms(d, gsub_ref[...]) * (1.0 - lam_init)).astype(BF16)


def _diff_decode(page_table, q, knew, lam_w, g_sub, cache, layer, lam_init):
    n_seq, n_pages = page_table.shape
    cp = _pages_per_step(n_pages, KV_DECODE_PAGES)
    nch = n_pages // cp
    nq = q.shape[1]
    assert nq == DIFF_HEADS * NEW_ROWS_PAD
    return pl.pallas_call(
        functools.partial(_diff_decode_kernel, layer=layer, cp=cp, nch=nch, lam_init=lam_init),
        out_shape=SDS((n_seq, nq, LANES), BF16),
        grid_spec=pltpu.PrefetchScalarGridSpec(
            num_scalar_prefetch=1,
            grid=(n_seq * nch,),
            in_specs=[pl.BlockSpec((None, nq, LANES), lambda c, pt: (c // nch, 0, 0)),
                      pl.BlockSpec((None, NEW_ROWS_PAD, 4 * LANES), lambda c, pt: (c // nch, 0, 0))]
                     + _lam_specs() + [pl.BlockSpec(memory_space=pl.ANY)],
            out_specs=pl.BlockSpec((None, nq, LANES), lambda c, pt: (c // nch, 0, 0)),
            scratch_shapes=[pltpu.VMEM((2, DIFF_KV_HEADS, cp * PAGE_SIZE, 2 * LANES), F32), pltpu.SemaphoreType.DMA((2,)),
                            pltpu.VMEM((2 * nq, LANES), F32), pltpu.VMEM((2 * nq, LANES), F32),
                            pltpu.VMEM((2 * nq, LANES), F32), pltpu.VMEM((2 * nq, LANES), F32)]),
        compiler_params=_params(("arbitrary",)),
        name="diff_decode",
    )(page_table, q, knew, *lam_w, g_sub, cache)


def _fox_decode_kernel(pt_ref, q_ref, kvnew_ref, lfnew_ref, cache_kv_ref, cache_lf_ref, o_ref,
                       bufkv, buflf, semkv, semlf, rterm_ref, carry_ref, m_ref, l_ref, acc_ref, *, layer, cp, nch):
    ps = _PageStream(pt_ref, cp, nch, reverse=True)
    per_kv = FOX_HEADS // FOX_KV_HEADS
    gq = per_kv * NEW_ROWS_PAD
    rows = FOX_HEADS * NEW_ROWS_PAD
    keys = cp * PAGE_SIZE

    def make_kv(page, p, slot):
        return [pltpu.make_async_copy(cache_kv_ref.at[layer, page, :, g, :],
                                      bufkv.at[slot, g, pl.ds(p * PAGE_SIZE, PAGE_SIZE), :], semkv.at[slot])
                for g in range(FOX_KV_HEADS)]

    def make_lf(page, p, slot):
        return [pltpu.make_async_copy(cache_lf_ref.at[layer, page], buflf.at[slot, :, pl.ds(p * PAGE_SIZE, PAGE_SIZE)],
                                      semlf.at[slot])]

    ps.prefetch(make_kv)
    ps.prefetch(make_lf)
    q = q_ref[...]

    def scores(kv_of_group):
        return jnp.concatenate([_dot_nt(q[gq * g:gq * (g + 1), :], kv_of_group(g)) for g in range(FOX_KV_HEADS)], axis=0)

    def values(kv_of_group):
        return lambda p: jnp.concatenate([_dot(p[gq * g:gq * (g + 1), :], kv_of_group(g))
                                          for g in range(FOX_KV_HEADS)], axis=0)

    def per_head_rows(x):
        return jnp.concatenate([jnp.broadcast_to(x[hd:hd + 1, :], (NEW_ROWS_PAD, x.shape[1])) for hd in range(FOX_HEADS)],
                               axis=0)

    @pl.when(ps.chunk == 0)
    def _():
        lf = lfnew_ref[...]
        r8 = lax.broadcasted_iota(I32, lf.shape, 0)
        cn = lf
        for sh in (1, 2, 4):
            cn = cn + jnp.where(r8 >= sh, pltpu.roll(cn, sh, 0), 0.0)
        cn = cn * LOG2E
        for hd in range(FOX_HEADS):
            rterm_ref[NEW_ROWS_PAD * hd:NEW_ROWS_PAD * (hd + 1), :] = jnp.broadcast_to(cn[:, hd:hd + 1],
                                                                                       (NEW_ROWS_PAD, LANES))
        carry_ref[...] = jnp.zeros(carry_ref.shape, F32)
        hh = lax.broadcasted_iota(I32, (FOX_HEADS, FOX_HEADS), 0)
        cc = lax.broadcasted_iota(I32, (FOX_HEADS, FOX_HEADS), 1)
        eye = jnp.where(hh == cc, 1.0, 0.0).astype(BF16)
        cnt = sum(_dot_nt(eye, part) for part in _split3(cn))
        bias_new = -per_head_rows(cnt)
        t = lax.broadcasted_iota(I32, (rows, NEW_ROWS_PAD), 0) & (NEW_ROWS_PAD - 1)
        u = lax.broadcasted_iota(I32, (rows, NEW_ROWS_PAD), 1)
        _init_softmax(m_ref, l_ref, acc_ref)
        kvn = lambda g: kvnew_ref[:, LANES * g:LANES * (g + 1)]
        s = jnp.where(u <= t, scores(kvn) + bias_new, NEG_BIG)
        _softmax_update(s, values(kvn), m_ref, l_ref, acc_ref, row_term=rterm_ref[...])

    ps.wait(make_kv)
    ps.wait(make_lf)

    blk = PREFIX_BLOCK
    rr = lax.broadcasted_iota(I32, (blk, blk), 0)
    cc2 = lax.broadcasted_iota(I32, (blk, blk), 1)
    tri = jnp.where(rr > cc2, 1.0, 0.0).astype(BF16)
    nblk = keys // blk
    local, total = [], []
    for sb in range(nblk):
        x = buflf[ps.slot, :, sb * blk:(sb + 1) * blk]
        hi, mid, lo = _split3(x)
        d = _dot(hi, tri) + _dot(mid, tri) + _dot(lo, tri)
        local.append(d)
        total.append(d[:, 0:1] + x[:, 0:1])
    carry = carry_ref[:, 0:1]
    pieces = [None] * nblk
    for sb in reversed(range(nblk)):
        pieces[sb] = local[sb] + carry
        carry = carry + total[sb]
    carry_ref[...] = jnp.broadcast_to(carry, carry_ref.shape)
    dk = jnp.concatenate(pieces, axis=1) * LOG2E
    bias = per_head_rows(dk)

    kvb = [bufkv[ps.slot, g].astype(BF16) for g in range(FOX_KV_HEADS)]
    s = scores(lambda g: kvb[g]) + bias
    _softmax_update(s, values(lambda g: kvb[g]), m_ref, l_ref, acc_ref, row_term=rterm_ref[...])

    @pl.when(ps.chunk == nch - 1)
    def _():
        o_ref[...] = (acc_ref[...] / l_ref[...]).astype(BF16)


def _fox_decode(page_table, q, kvnew, lfnew, cache_kv, cache_lf_t, layer):
    n_seq, n_pages = page_table.shape
    cp = _pages_per_step(n_pages, KV_DECODE_PAGES)
    nch = n_pages // cp
    rows = q.shape[1]
    assert rows == FOX_HEADS * NEW_ROWS_PAD
    keys = cp * PAGE_SIZE
    nkv = FOX_KV_HEADS
    return pl.pallas_call(
        functools.partial(_fox_decode_kernel, layer=layer, cp=cp, nch=nch),
        out_shape=SDS((n_seq, rows, LANES), BF16),
        grid_spec=pltpu.PrefetchScalarGridSpec(
            num_scalar_prefetch=1,
            grid=(n_seq * nch,),
            in_specs=[pl.BlockSpec((None, rows, LANES), lambda c, pt: (c // nch, 0, 0)),
                      pl.BlockSpec((None, NEW_ROWS_PAD, 4 * LANES), lambda c, pt: (c // nch, 0, 0)),
                      pl.BlockSpec((None, NEW_ROWS_PAD, FOX_HEADS), lambda c, pt: (c // nch, 0, 0)),
                      pl.BlockSpec(memory_space=pl.ANY), pl.BlockSpec(memory_space=pl.ANY)],
            out_specs=pl.BlockSpec((None, rows, LANES), lambda c, pt: (c // nch, 0, 0)),
            scratch_shapes=[pltpu.VMEM((2, nkv, keys, LANES), F32), pltpu.VMEM((2, FOX_HEADS, keys), F32),
                            pltpu.SemaphoreType.DMA((2,)), pltpu.SemaphoreType.DMA((2,)),
                            pltpu.VMEM((rows, LANES), F32), pltpu.VMEM((FOX_HEADS, LANES), F32),
                            pltpu.VMEM((rows, LANES), F32), pltpu.VMEM((rows, LANES), F32),
                            pltpu.VMEM((rows, LANES), F32)]),
        compiler_params=_params(("arbitrary",)),
        name="fox_decode",
    )(page_table, q, kvnew, lfnew, cache_kv, cache_lf_t)


def _pad_new_rows(a, n_seq, t_new):
    a = a.reshape(n_seq, t_new, a.shape[-1])
    return jnp.pad(a, ((0, 0), (0, NEW_ROWS_PAD - t_new), (0, 0)))


def _heads_major(a, n_seq, t_new, heads):
    a = a.reshape(n_seq, t_new, heads, LANES).swapaxes(1, 2)
    a = jnp.pad(a, ((0, 0), (0, 0), (0, NEW_ROWS_PAD - t_new), (0, 0)))
    return a.reshape(n_seq, heads * NEW_ROWS_PAD, LANES)


def _tokens_major(o, n_seq, t_new, heads):
    return o.reshape(n_seq, heads, NEW_ROWS_PAD, LANES)[:, :, :t_new].swapaxes(1, 2)


def _pick_tile(n, pref):
    t = min(pref, n)
    while n % t:
        t //= 2
    return t


def _trunk(x, mods, tok, tabs, n_seq, seq_len, weights, depth, paged):
    alpha = (2 * depth) ** 0.25
    rows_mla, rows_diff, rows_kv, rows_lf = [], [], [], []
    e_mat = _fox_lane_matrices()
    tq = _pick_tile(seq_len, ATTN_TQ)
    wide = max(tq, _pick_tile(seq_len, ATTN_WIDE))
    for l in range(depth):
        mod = mods[l]
        i = l // 2
        if l % 2 == 0:
            w = weights['even'][i]
            lam_init = 0.8 - 0.6 * math.exp(-0.3 * l)
            qmla, kmla, mrows, dq, drows, dkv = _even_in(tok, x, mod, tabs, w, weights['mla_g_q'][i], weights['mla_g_kv'][i])
            lam_w = [weights[k][i].reshape(1, DIFF_HD) for k in ('diff_lam_q1', 'diff_lam_k1', 'diff_lam_q2', 'diff_lam_k2')]
            g_sub = weights['diff_g_sub'][i].reshape(1, 2 * DIFF_HD)
            if paged is None:
                lat = _mla_prompt(qmla, kmla, n_seq, seq_len, tq, wide)
                odif = _diff_prompt(dq, dkv, lam_w, g_sub, n_seq, seq_len, tq, wide, lam_init)
            else:
                page_table, cache_mla_t, cache_diff = paged[0], paged[1], paged[2]
                lat = _mla_decode(page_table, qmla.reshape(n_seq, seq_len * MLA_HEADS, MLA_KW),
                                  _pad_new_rows(kmla, n_seq, seq_len), cache_mla_t, i)
                lat = lat.reshape(tok.n, MLA_HEADS * KV_LORA)
                odif = _diff_decode(page_table, _heads_major(dq, n_seq, seq_len, DIFF_HEADS),
                                    _pad_new_rows(dkv, n_seq, seq_len), lam_w, g_sub, cache_diff, i, lam_init)
                odif = _tokens_major(odif, n_seq, seq_len, DIFF_HEADS).reshape(tok.n, DIFF_HEADS * LANES)
            x = _mixer_out_mlp(tok, x, mod, [lat, odif], [w['w_uv'], w['w_out_mla'], w['w_out_dif']],
                               weights['mlp_w_up'][l], weights['mlp_w_down'][l], weights['ln_g'][l], weights['ln_b'][l],
                               alpha, even=True)
            rows_mla.append(mrows)
            rows_diff.append(drows)
        else:
            w = weights['odd'][i]
            q, kvrows, kvb, logf = _odd_in(tok, x, mod, w, weights['fox_b_f'][i])
            if paged is None:
                augk, prow = _fox_prefix(logf, n_seq, seq_len, e_mat)
                o_fox = _fox_prompt(q, kvb, augk, prow, n_seq, seq_len, tq, wide)
            else:
                page_table, cache_kv, cache_lf_t = paged[0], paged[3], paged[4]
                o = _fox_decode(page_table, _heads_major(q, n_seq, seq_len, FOX_HEADS),
                                _pad_new_rows(kvb, n_seq, seq_len), _pad_new_rows(logf, n_seq, seq_len),
                                cache_kv, cache_lf_t, i)
                o_fox = _tokens_major(o, n_seq, seq_len, FOX_HEADS)[..., FOX_HD:].reshape(tok.n, FOX_HEADS * FOX_HD)
            x = _mixer_out_mlp(tok, x, mod, [o_fox], [w['w_out']],
                               weights['mlp_w_up'][l], weights['mlp_w_down'][l], weights['ln_g'][l], weights['ln_b'][l],
                               alpha, even=False)
            rows_kv.append(kvrows)
            rows_lf.append(logf)
    return x, rows_mla, rows_diff, rows_kv, rows_lf


def kernel(x_prompt, x_sample, cache_mla, cache_diff, cache_fox_kv, cache_fox_logf, page_table, c_prompt, c_sample, even_w_in, mla_g_q, mla_w_uq, mla_g_kv, mla_w_uk, mla_w_uv, diff_lam_q1, diff_lam_k1, diff_lam_q2, diff_lam_k2, diff_g_sub, even_w_out, fox_w_in, fox_b_f, fox_w_out, ada_w, ada_b, ln_g, ln_b, mlp_w_up, mlp_w_down):
    depth = ada_w.shape[0]
    bp, sp, d = x_prompt.shape
    bs, ts, _ = x_sample.shape
    n_pages = page_table.shape[1]
    past_len = n_pages * PAGE_SIZE
    assert d == D_MODEL and ts <= NEW_ROWS_PAD and cache_mla.shape[2] == PAGE_SIZE

    weights = dict(
        even=[_prep_even_weights(even_w_in[i], mla_w_uq[i], mla_w_uk[i], mla_w_uv[i], even_w_out[i])
              for i in range(even_w_in.shape[0])],
        odd=[_prep_odd_weights(fox_w_in[i], fox_w_out[i]) for i in range(fox_w_in.shape[0])],
        mla_g_q=mla_g_q, mla_g_kv=mla_g_kv, diff_lam_q1=diff_lam_q1, diff_lam_k1=diff_lam_k1,
        diff_lam_q2=diff_lam_q2, diff_lam_k2=diff_lam_k2, diff_g_sub=diff_g_sub, fox_b_f=fox_b_f,
        ln_g=ln_g, ln_b=ln_b, mlp_w_up=mlp_w_up.astype(BF16), mlp_w_down=mlp_w_down.astype(BF16))

    mod = _adaln(jnp.concatenate([c_prompt, c_sample], 0), ada_w, ada_b)
    mods_p = [mod[l, :bp].reshape(bp, 1, 6 * d) for l in range(depth)]
    mods_s = [jnp.repeat(mod[l, bp:], ts, axis=0).reshape(1, bs * ts, 6 * d) for l in range(depth)]

    tm_p = _pick_tile(sp, TOKEN_TILE)
    tok_p = _Tokens(bp * sp, tm_p, sp // tm_p, per_token=False)
    pos_p = jnp.arange(sp)
    tabs_p = _rope_tables(pos_p, DIFF_HD // 2) + _rope_tables(pos_p, MLA_ROPE // 2)
    y_p, mla_p, diff_p, kv_p, lf_p = _trunk(x_prompt.reshape(bp * sp, d), mods_p, tok_p, tabs_p, bp, sp,
                                            weights, depth, None)

    n_s = bs * ts
    tm_s = _pick_tile(n_s, 512)
    tok_s = _Tokens(n_s, tm_s, 1, per_token=True)
    pos_s = jnp.tile(past_len + jnp.arange(ts), bs)
    tabs_s = _rope_tables(pos_s, DIFF_HD // 2) + _rope_tables(pos_s, MLA_ROPE // 2)
    paged = (page_table, jnp.swapaxes(cache_mla, 2, 3), cache_diff, cache_fox_kv, jnp.swapaxes(cache_fox_logf, 2, 3))
    y_s, mla_s, diff_s, kv_s, lf_s = _trunk(x_sample.reshape(n_s, d), mods_s, tok_s, tabs_s, bs, ts,
                                            weights, depth, paged)

    def stack(rows, lead, tail):
        return jnp.stack([r.reshape(lead + tail) for r in rows])

    return (y_p.reshape(bp, sp, d), y_s.reshape(bs, ts, d),
            stack(mla_p, (bp, sp), (MLA_KUSED,)), stack(mla_s, (bs, ts), (MLA_KUSED,)),
            stack(diff_p, (bp, sp), (DIFF_KV_HEADS, 4 * DIFF_HD)), stack(diff_s, (bs, ts), (DIFF_KV_HEADS, 4 * DIFF_HD)),
            stack(kv_p, (bp, sp), (FOX_KV_HEADS, 2 * FOX_HD)), stack(kv_s, (bs, ts), (FOX_KV_HEADS, 2 * FOX_HD)),
            stack(lf_p, (bp, sp), (FOX_HEADS,)), stack(lf_s, (bs, ts), (FOX_HEADS,)))
```
